```python
import math
import jax
import jax.numpy as jnp
from jax import lax
import numpy as np

D_MODEL = 1024
BATCH = 8
SEQ = 8192
DEPTH = 2

GRID_W = 64
CTX_LEN = 256
RMS_EPS = 1e-6

N_DIFF_HEADS = 4
DIFF_HEAD_DIM = 64
DIFF_V_DIM = 2 * DIFF_HEAD_DIM
ATT_WIDTH = N_DIFF_HEADS * DIFF_V_DIM
Q_BLOCK = 128
ROPE_THETA = 10000.0
ROPE_FREQS = DIFF_HEAD_DIM // 4

S5_WIDTH = D_MODEL // 4
S5_GROUP = 16
S5_GROUPS = S5_WIDTH // S5_GROUP
S5_STATE = 64
S5_DT_MIN = 1e-3
S5_DT_MAX = 1e-1

GMLP_WIDTH = D_MODEL // 4
GMLP_HEADS = 4
GMLP_HEAD_DIM = GMLP_WIDTH // GMLP_HEADS
CHUNK = 128

N_BRANCH = 3

K_COLS = N_DIFF_HEADS * 2 * DIFF_HEAD_DIM
V_COLS = ATT_WIDTH
S5_COLS = S5_WIDTH
Q_COLS = N_DIFF_HEADS * 2 * DIFF_HEAD_DIM
GMLP_COLS = 2 * GMLP_WIDTH
GATE_COLS = N_BRANCH * D_MODEL
K_END = K_COLS
V_END = K_END + V_COLS
S5_END = V_END + S5_COLS
CTX_SIDE_COLS = S5_END
Q_END = S5_END + Q_COLS
GMLP_END = Q_END + GMLP_COLS
IN_COLS = GMLP_END + GATE_COLS

N_EXPERTS = 16
EC_CAPACITY_FACTOR = 2
EXPERT_FF = 1536

kernel_name = 'hybrid_diffattn_s5_gmlp_ecmoe'


def rms_norm(x, g):
    xf = x.astype(jnp.float32)
    y = xf * lax.rsqrt(jnp.mean(xf * xf, axis=-1, keepdims=True) + RMS_EPS)
    return (y * g.astype(jnp.float32)).astype(x.dtype)


def layer_norm(x, g):
    xf = x.astype(jnp.float32)
    mu = jnp.mean(xf, axis=-1, keepdims=True)
    xc = xf - mu
    y = xc * lax.rsqrt(jnp.mean(xc * xc, axis=-1, keepdims=True) + RMS_EPS)
    return (y * g.astype(jnp.float32)).astype(x.dtype)


def modulate(x, shift, scale):
    return x * (1.0 + scale) + shift


def axial_rope_tables(n_tokens):
    rows = n_tokens // GRID_W
    row = jnp.repeat(jnp.arange(rows), GRID_W).astype(jnp.float32)
    col = jnp.tile(jnp.arange(GRID_W), rows).astype(jnp.float32)
    inv = ROPE_THETA ** (-jnp.arange(ROPE_FREQS, dtype=jnp.float32) / ROPE_FREQS)
    ang = jnp.stack([row[:, None] * inv, col[:, None] * inv], axis=1)
    return jnp.cos(ang), jnp.sin(ang)


def apply_rope(x, cos, sin):
    shp = x.shape
    xr = x.reshape(shp[:-1] + (2, 2, ROPE_FREQS)).astype(jnp.float32)
    x0, x1 = xr[..., 0, :], xr[..., 1, :]
    c = cos[None, :, None, None]
    s = sin[None, :, None, None]
    out = jnp.stack([x0 * c - x1 * s, x1 * c + x0 * s], axis=-2)
    return out.reshape(shp).astype(x.dtype)


def diff_attention_block(q, k, v, lam):
    s = jnp.einsum('bqhmd,bkhmd->bhmqk', q, k, preferred_element_type=jnp.float32) * (DIFF_HEAD_DIM ** -0.5)
    p = jax.nn.softmax(s, axis=-1)
    a = p[:, :, 0] - lam * p[:, :, 1]
    return jnp.einsum('bhqk,bkhe->bqhe', a.astype(v.dtype), v)


def latent_diff_attention(q, k_all, v_all, lam):
    b, n, h, m, d = q.shape
    nblk = n // Q_BLOCK
    qb = q.reshape(b, nblk, Q_BLOCK, h, m, d).swapaxes(0, 1)
    out = lax.map(lambda blk: diff_attention_block(blk, k_all, v_all, lam), qb)
    return out.swapaxes(0, 1).reshape(b, n, h, DIFF_V_DIM)


def diff_head_out(o, g, lam_init):
    b, n = o.shape[:2]
    return (rms_norm(o, g) * (1.0 - lam_init)).reshape(b, n, ATT_WIDTH)


def s5_discretize(lam_re, lam_im, log_dt):
    dt = jnp.exp(log_dt.astype(jnp.float32))[:, None]
    lr = lam_re.astype(jnp.float32)
    li = lam_im.astype(jnp.float32)
    mag = jnp.exp(lr * dt)
    ar = mag * jnp.cos(li * dt)
    ai = mag * jnp.sin(li * dt)
    den = lr * lr + li * li
    fr = ((ar - 1.0) * lr + ai * li) / den
    fi = (ai * lr - (ar - 1.0) * li) / den
    return ar, ai, fr, fi


def complex_affine_combine(e1, e2):
    a1r, a1i, b1r, b1i = e1
    a2r, a2i, b2r, b2i = e2
    return (a2r * a1r - a2i * a1i,
            a2r * a1i + a2i * a1r,
            a2r * b1r - a2i * b1i + b2r,
            a2r * b1i + a2i * b1r + b2i)


def s5_scan(u, disc, b_dir, s0):
    ar, ai, fr, fi = disc
    uf = u.astype(jnp.float32)
    bu_re = jnp.einsum('blgc,gnc->blgn', uf, b_dir[0].astype(jnp.float32))
    bu_im = jnp.einsum('blgc,gnc->blgn', uf, b_dir[1].astype(jnp.float32))
    x_re = fr * bu_re - fi * bu_im
    x_im = fr * bu_im + fi * bu_re
    if s0 is not None:
        s0_re, s0_im = s0
        x_re = x_re.at[:, 0].add(ar * s0_re - ai * s0_im)
        x_im = x_im.at[:, 0].add(ar * s0_im + ai * s0_re)
    n = u.shape[1]
    a_re = jnp.broadcast_to(ar, (1, n) + ar.shape)
    a_im = jnp.broadcast_to(ai, (1, n) + ai.shape)
    _, _, s_re, s_im = lax.associative_scan(complex_affine_combine, (a_re, a_im, x_re, x_im), axis=1)
    return s_re, s_im


def s5_readout(s_re, s_im, c_re, c_im):
    return (jnp.einsum('gcn,blgn->blgc', c_re.astype(jnp.float32), s_re)
            - jnp.einsum('gcn,blgn->blgc', c_im.astype(jnp.float32), s_im))


def s5_bidirectional(u_lat, u_ctx, lam_re, lam_im, log_dt, b_re, b_im, c_re, c_im, d_skip, need_ctx_out):
    d_g = d_skip.astype(jnp.float32).reshape(S5_GROUPS, S5_GROUP)
    y_lat = d_g * u_lat.astype(jnp.float32)
    y_ctx = d_g * u_ctx.astype(jnp.float32) if need_ctx_out else None
    for direction in range(2):
        if direction == 1:
            orient = lambda t: jnp.flip(t, axis=1)
        else:
            orient = lambda t: t
        disc = s5_discretize(lam_re[direction], lam_im[direction], log_dt[direction])
        b_dir = (b_re[direction], b_im[direction])
        sc_re, sc_im = s5_scan(orient(u_ctx), disc, b_dir, None)
        sl_re, sl_im = s5_scan(orient(u_lat), disc, b_dir, (sc_re[:, -1], sc_im[:, -1]))
        y_lat = y_lat + orient(s5_readout(sl_re, sl_im, c_re[direction], c_im[direction]))
        if need_ctx_out:
            y_ctx = y_ctx + orient(s5_readout(sc_re, sc_im, c_re[direction], c_im[direction]))
    return y_lat, y_ctx


def gmlp_spatial_gating(z, norm_g, w_s, b_s):
    zu, zv = jnp.split(z, 2, axis=-1)
    zv = layer_norm(zv, norm_g)
    b, n, _ = zv.shape
    zv = zv.reshape(b, n // CHUNK, CHUNK, GMLP_HEADS, GMLP_HEAD_DIM)
    mixed = jnp.einsum('gpq,bkqgc->bkpgc', w_s, zv) + b_s.T[None, None, :, :, None]
    return zu * mixed.reshape(b, n, GMLP_WIDTH)


def context_side_columns(p):
    b, n = p.shape[:2]
    k = p[..., :K_END].reshape(b, n, N_DIFF_HEADS, 2, DIFF_HEAD_DIM)
    v = p[..., K_END:V_END].reshape(b, n, N_DIFF_HEADS, DIFF_V_DIM)
    u = p[..., V_END:S5_END].reshape(b, n, S5_GROUPS, S5_GROUP)
    return k, v, u


def query_side_columns(p):
    b, n = p.shape[:2]
    q = p[..., S5_END:Q_END].reshape(b, n, N_DIFF_HEADS, 2, DIFF_HEAD_DIM)
    zg = jax.nn.gelu(p[..., Q_END:GMLP_END])
    gates = jax.nn.sigmoid(p[..., GMLP_END:IN_COLS]).reshape(b, n, N_BRANCH, D_MODEL)
    return q, zg, gates


def merge_branches(o_att, y_s5, o_gm, gates, w_br_att, w_s5_glu, w_br_gmlp, w_out):
    y_att = o_att @ w_br_att
    z = jax.nn.gelu(y_s5) @ w_s5_glu
    y5 = z[..., :D_MODEL] * jax.nn.sigmoid(z[..., D_MODEL:])
    y_gm = o_gm @ w_br_gmlp
    merged = gates[..., 0, :] * y_att + gates[..., 1, :] * y5 + gates[..., 2, :] * y_gm
    return merged @ w_out


def ec_moe(h, w_router, w_gate, w_up, w_down):
    b, n, _ = h.shape
    cap = EC_CAPACITY_FACTOR * n // N_EXPERTS
    probs = jax.nn.softmax(jnp.einsum('bnd,de->bne', h, w_router).astype(jnp.float32), axis=-1)
    gate, idx = lax.top_k(jnp.swapaxes(probs, 1, 2), cap)
    bidx = jnp.arange(b)[:, None, None]
    xe = h[bidx, idx]
    hid = jax.nn.silu(jnp.einsum('becd,edf->becf', xe, w_gate)) * jnp.einsum('becd,edf->becf', xe, w_up)
    ye = jnp.einsum('becf,efd->becd', hid, w_down) * gate[..., None].astype(h.dtype)
    return jnp.zeros_like(h).at[bidx, idx].add(ye)


def setup_inputs(seed: int = 0) -> dict:
    key = jax.random.key(seed)
    ks = jax.random.split(key, 32)
    f32 = jnp.float32

    def nrm(k, shape, scale):
        return jax.random.normal(k, shape, f32) * scale

    L = DEPTH
    D = D_MODEL
    E = N_EXPERTS
    F = EXPERT_FF
    G = S5_GROUPS
    N = S5_STATE
    P = S5_GROUP
    n_idx = jnp.arange(N, dtype=f32)
    return {
        'x': nrm(ks[0], (BATCH, SEQ, D), 1.0),
        'c': nrm(ks[1], (BATCH, D), 1.0),
        'ctx': nrm(ks[2], (BATCH, CTX_LEN, D), 1.0),
        'c_ctx': nrm(ks[3], (D,), 1.0),
        'w_mod': nrm(ks[4], (L, D, 6 * D), 0.5 * D ** -0.5),
        'b_mod': nrm(ks[5], (L, 6 * D), 0.01),
        'norm_g': 1.0 + nrm(ks[6], (L, 4, D), 0.02),
        'w_in': nrm(ks[7], (L, D, IN_COLS), D ** -0.5),
        'diff_lambda': nrm(ks[8], (L, 4, DIFF_HEAD_DIM), 0.1),
        'diff_subln_g': 1.0 + nrm(ks[9], (L, DIFF_V_DIM), 0.02),
        's5_lam_re': -0.5 + nrm(ks[10], (L, 2, G, N), 0.01),
        's5_lam_im': math.pi * n_idx + nrm(ks[11], (L, 2, G, N), 0.01),
        's5_log_dt': jax.random.uniform(ks[12], (L, 2, G), f32, math.log(S5_DT_MIN), math.log(S5_DT_MAX)),
        's5_b_re': nrm(ks[13], (L, 2, G, N, P), (2 * P) ** -0.5),
        's5_b_im': nrm(ks[14], (L, 2, G, N, P), (2 * P) ** -0.5),
        's5_c_re': nrm(ks[15], (L, 2, G, P, N), 0.5 ** 0.5),
        's5_c_im': nrm(ks[16], (L, 2, G, P, N), 0.5 ** 0.5),
        's5_d': nrm(ks[17], (L, S5_WIDTH), 1.0),
        'w_s5_glu': nrm(ks[18], (L, S5_WIDTH, 2 * D), S5_WIDTH ** -0.5),
        'gmlp_norm_g': 1.0 + nrm(ks[19], (L, GMLP_WIDTH), 0.02),
        'gmlp_ws': nrm(ks[20], (L, GMLP_HEADS, CHUNK, CHUNK), CHUNK ** -0.5),
        'gmlp_bs': 1.0 + nrm(ks[21], (L, GMLP_HEADS, CHUNK), 0.02),
        'w_br_att': nrm(ks[22], (L, ATT_WIDTH, D), ATT_WIDTH ** -0.5),
        'w_br_gmlp': nrm(ks[23], (L, GMLP_WIDTH, D), GMLP_WIDTH ** -0.5),
        'w_out': nrm(ks[24], (L, D, D), D ** -0.5),
        'w_router': nrm(ks[25], (L, D, E), D ** -0.5),
        'w_e_gate': nrm(ks[26], (L, E, D, F), D ** -0.5),
        'w_e_up': nrm(ks[27], (L, E, D, F), D ** -0.5),
        'w_e_down': nrm(ks[28], (L, E, F, D), F ** -0.5),
    }


def reference(x, c, ctx, c_ctx, w_mod, b_mod, norm_g, w_in, diff_lambda, diff_subln_g,
              s5_lam_re, s5_lam_im, s5_log_dt, s5_b_re, s5_b_im, s5_c_re, s5_c_im, s5_d,
              w_s5_glu, gmlp_norm_g, gmlp_ws, gmlp_bs, w_br_att, w_br_gmlp, w_out,
              w_router, w_e_gate, w_e_up, w_e_down):
    b, n, _ = x.shape
    n_ctx = ctx.shape[1]
    cos, sin = axial_rope_tables(n)
    xc = ctx
    for l in range(DEPTH):
        last = l == DEPTH - 1
        lam_init = 0.8 - 0.6 * math.exp(-0.3 * l)
        mod = jax.nn.silu(c) @ w_mod[l] + b_mod[l]
        mod_c = jax.nn.silu(c_ctx) @ w_mod[l] + b_mod[l]
        sh1, sc1, g1, sh2, sc2, g2 = jnp.split(mod[:, None, :], 6, axis=-1)
        csh1, csc1, cg1, csh2, csc2, cg2 = jnp.split(mod_c, 6, axis=-1)

        h = modulate(rms_norm(x, norm_g[l, 0]), sh1, sc1)
        hc = modulate(rms_norm(xc, norm_g[l, 0]), csh1, csc1)
        proj = h @ w_in[l]
        proj_c = hc @ (w_in[l, :, :CTX_SIDE_COLS] if last else w_in[l])
        k, v, u = context_side_columns(proj)
        q, zg, gates = query_side_columns(proj)
        kc, vc, uc = context_side_columns(proj_c)

        lq1, lk1, lq2, lk2 = diff_lambda[l].astype(jnp.float32)
        lam = jnp.exp(jnp.sum(lq1 * lk1)) - jnp.exp(jnp.sum(lq2 * lk2)) + lam_init
        k_all = jnp.concatenate([kc, apply_rope(k, cos, sin)], axis=1)
        v_all = jnp.concatenate([vc, v], axis=1)
        o_att = diff_head_out(latent_diff_attention(apply_rope(q, cos, sin), k_all, v_all, lam),
                              diff_subln_g[l], lam_init)

        y5, y5c = s5_bidirectional(u, uc, s5_lam_re[l], s5_lam_im[l], s5_log_dt[l],
                                   s5_b_re[l], s5_b_im[l], s5_c_re[l], s5_c_im[l], s5_d[l],
                                   need_ctx_out=not last)

        o_gm = gmlp_spatial_gating(zg, gmlp_norm_g[l], gmlp_ws[l], gmlp_bs[l])

        mix = merge_branches(o_att, y5.reshape(b, n, S5_WIDTH).astype(x.dtype), o_gm, gates,
                             w_br_att[l], w_s5_glu[l], w_br_gmlp[l], w_out[l])
        x = x + g1 * rms_norm(mix, norm_g[l, 1])
        if not last:
            qc, zgc, gates_c = query_side_columns(proj_c)
            o_att_c = diff_head_out(diff_attention_block(qc, kc, vc, lam), diff_subln_g[l], lam_init)
            o_gm_c = gmlp_spatial_gating(zgc, gmlp_norm_g[l], gmlp_ws[l], gmlp_bs[l])
            mix_c = merge_branches(o_att_c, y5c.reshape(b, n_ctx, S5_WIDTH).astype(xc.dtype), o_gm_c,
                                   gates_c, w_br_att[l], w_s5_glu[l], w_br_gmlp[l], w_out[l])
            xc = xc + cg1 * rms_norm(mix_c, norm_g[l, 1])

        h2 = modulate(rms_norm(x, norm_g[l, 2]), sh2, sc2)
        y_moe = ec_moe(h2, w_router[l], w_e_gate[l], w_e_up[l], w_e_down[l])
        x = x + g2 * rms_norm(y_moe, norm_g[l, 3])
        if not last:
            h2c = modulate(rms_norm(xc, norm_g[l, 2]), csh2, csc2)
            y_moe_c = ec_moe(h2c, w_router[l], w_e_gate[l], w_e_up[l], w_e_down[l])
            xc = xc + cg2 * rms_norm(y_moe_c, norm_g[l, 3])
    return x
```

```python
import functools
import math

import jax
import jax.numpy as jnp
from jax import lax
from jax.experimental import pallas as pl
from jax.experimental.pallas import tpu as pltpu

F32 = jnp.float32
BF16 = jnp.bfloat16

D_MODEL = 1024
GRID_W = 64
RMS_EPS = 1e-6
N_HEADS = 4
HEAD_DIM = 64
V_DIM = 2 * HEAD_DIM
ATT_W = N_HEADS * V_DIM
ROPE_THETA = 10000.0
ROPE_FREQS = HEAD_DIM // 4
S5_W = D_MODEL // 4
S5_P = 16
S5_G = S5_W // S5_P
S5_N = 64
S5_LANES = S5_G * S5_N
GM_W = D_MODEL // 4
GM_H = 4
GM_HD = GM_W // GM_H
CHUNK = 128
N_EXP = 16
CAP_FACTOR = 2
K_END = ATT_W
V_END = K_END + ATT_W
S5_END = V_END + S5_W
Q_END = S5_END + ATT_W
GMLP_END = Q_END + 2 * GM_W
IN_COLS = GMLP_END + 3 * D_MODEL
LOG2E = 1.4426950408889634

V7X_VMEM_BYTES = 64 * 1024 * 1024
VMEM_LIMIT = 56 * 1024 * 1024
FF_CHUNK = 512
MOD_ROWS = 16


def _params(sem):
    return pltpu.CompilerParams(dimension_semantics=sem, vmem_limit_bytes=VMEM_LIMIT)


def _mod_body(c_ref, w_ref, b_ref, o_ref):
    c = c_ref[...]
    s = (c * jax.nn.sigmoid(c)).astype(BF16)
    o_ref[0] = jnp.dot(s, w_ref[0].astype(BF16), preferred_element_type=F32) + b_ref[0]


def modulation(c_rows, w_mod, b_mod, tn=1536):
    depth, d, n6 = w_mod.shape
    return pl.pallas_call(
        _mod_body,
        grid=(depth, n6 // tn),
        in_specs=[
            pl.BlockSpec((MOD_ROWS, d), lambda l, j: (0, 0)),
            pl.BlockSpec((1, d, tn), lambda l, j: (l, 0, j)),
            pl.BlockSpec((1, 1, tn), lambda l, j: (l, 0, j)),
        ],
        out_specs=pl.BlockSpec((1, MOD_ROWS, tn), lambda l, j: (l, 0, j)),
        out_shape=jax.ShapeDtypeStruct((depth, MOD_ROWS, n6), F32),
        compiler_params=_params(("arbitrary", "arbitrary")),
        name="modulation",
    )(c_rows, w_mod, b_mod.reshape(depth, 1, n6))


def _rope_cols(p, cos, sin):
    lane = lax.broadcasted_iota(jnp.int32, (p.shape[0], 128), 1)
    first = (lane % 32) < ROPE_FREQS
    outs = []
    for s in range(p.shape[1] // 128):
        ps = p[:, s * 128:(s + 1) * 128]
        sw = jnp.where(first, pltpu.roll(ps, 128 - ROPE_FREQS, 1), pltpu.roll(ps, ROPE_FREQS, 1))
        outs.append(ps * cos + sw * sin)
    return jnp.concatenate(outs, axis=1)


def _inproj_body(full, x_ref, mod_ref, ng_ref, w_ref, cos_ref, sin_ref, gg_ref, ws_ref, bs_ref, *outs):
    xt = x_ref[0]
    h = xt * lax.rsqrt(jnp.mean(xt * xt, axis=-1, keepdims=True) + RMS_EPS) * ng_ref[...]
    h = h * (1.0 + mod_ref[0, 1:2, :]) + mod_ref[0, 0:1, :]
    hb = h.astype(BF16)

    def proj(a, b):
        return jnp.dot(hb, w_ref[:, a:b], preferred_element_type=F32)

    cos = cos_ref[...]
    sin = sin_ref[...]
    if full:
        q_ref, k_ref, v_ref, u_ref, ogm_ref, gate_ref = outs
    else:
        k_ref, v_ref, u_ref = outs
    k_ref[0] = _rope_cols(proj(0, K_END), cos, sin).astype(BF16)
    v_ref[0] = proj(K_END, V_END).astype(BF16)
    u_ref[0] = proj(V_END, S5_END)
    if not full:
        return
    q_ref[0] = (_rope_cols(proj(S5_END, Q_END), cos, sin) * (HEAD_DIM ** -0.5 * LOG2E)).astype(BF16)

    zg = jax.nn.gelu(proj(Q_END, GMLP_END))
    zu = zg[:, :GM_W]
    zv = zg[:, GM_W:]
    mu = jnp.mean(zv, axis=-1, keepdims=True)
    zc = zv - mu
    zvn = (zc * lax.rsqrt(jnp.mean(zc * zc, axis=-1, keepdims=True) + RMS_EPS) * gg_ref[...]).astype(BF16)
    head = lax.broadcasted_iota(jnp.int32, (CHUNK, GM_W), 1) // GM_HD
    tm = xt.shape[0]
    for c in range(tm // CHUNK):
        zc_blk = zvn[c * CHUNK:(c + 1) * CHUNK, :]
        mixed = bs_ref[...]
        for g in range(GM_H):
            mixed = mixed + jnp.dot(ws_ref[g], jnp.where(head == g, zc_blk, jnp.zeros_like(zc_blk)),
                                    preferred_element_type=F32)
        ogm_ref[0, c * CHUNK:(c + 1) * CHUNK, :] = (zu[c * CHUNK:(c + 1) * CHUNK, :] * mixed).astype(BF16)

    for j in range(3):
        a = GMLP_END + j * D_MODEL
        gate_ref[0, :, j * D_MODEL:(j + 1) * D_MODEL] = jax.nn.sigmoid(proj(a, a + D_MODEL)).astype(BF16)


def input_projection(x, modv, ng, w_in, cos_t, sin_t, gg, ws, bs, full, tm):
    b, n, d = x.shape
    assert n % tm == 0 and tm % CHUNK == 0
    row = lambda bb, j: (bb, j, 0)
    const2 = lambda bb, j: (0, 0)
    out_shape = [jax.ShapeDtypeStruct((b, n, ATT_W), BF16),
                 jax.ShapeDtypeStruct((b, n, ATT_W), BF16),
                 jax.ShapeDtypeStruct((b, n, S5_W), F32)]
    out_specs = [pl.BlockSpec((1, tm, ATT_W), row), pl.BlockSpec((1, tm, ATT_W), row),
                 pl.BlockSpec((1, tm, S5_W), row)]
    if full:
        out_shape = [jax.ShapeDtypeStruct((b, n, ATT_W), BF16)] + out_shape + [
            jax.ShapeDtypeStruct((b, n, GM_W), BF16), jax.ShapeDtypeStruct((b, n, 3 * d), BF16)]
        out_specs = [pl.BlockSpec((1, tm, ATT_W), row)] + out_specs + [
            pl.BlockSpec((1, tm, GM_W), row), pl.BlockSpec((1, tm, 3 * d), row)]
    return pl.pallas_call(
        functools.partial(_inproj_body, full),
        grid=(b, n // tm),
        in_specs=[
            pl.BlockSpec((1, tm, d), row),
            pl.BlockSpec((1, 6, d), lambda bb, j: (bb, 0, 0)),
            pl.BlockSpec((1, d), const2),
            pl.BlockSpec((d, IN_COLS), const2),
            pl.BlockSpec((tm, 128), lambda bb, j: (j, 0)),
            pl.BlockSpec((tm, 128), lambda bb, j: (j, 0)),
            pl.BlockSpec((1, GM_W), const2),
            pl.BlockSpec((GM_H, CHUNK, CHUNK), lambda bb, j: (0, 0, 0)),
            pl.BlockSpec((CHUNK, GM_W), const2),
        ],
        out_specs=out_specs,
        out_shape=out_shape,
        compiler_params=_params(("arbitrary", "arbitrary")),
        name="input_projection" if full else "input_projection_ctx_side",
    )(x, modv, ng, w_in, cos_t, sin_t, gg, ws, bs)


def _attn_body(lam_init, tk, has_latent, q_ref, kc_ref, vc_ref, k_ref, v_ref, dl_ref, sg_ref, o_ref,
               m_ref, l_ref, acc_ref):
    tq = q_ref.shape[1]
    q = q_ref[0]
    lane = lax.broadcasted_iota(jnp.int32, q.shape, 1)
    zero = jnp.zeros_like(q)
    q2 = jnp.concatenate([jnp.where(lane < HEAD_DIM, q, zero), jnp.where(lane >= HEAD_DIM, q, zero)], axis=0)
    m_ref[...] = jnp.full(m_ref.shape, -jnp.inf, F32)
    l_ref[...] = jnp.zeros(l_ref.shape, F32)
    acc_ref[...] = jnp.zeros(acc_ref.shape, F32)

    def step(kb, vb):
        s = lax.dot_general(q2, kb, (((1,), (1,)), ((), ())), preferred_element_type=F32)
        m_old = m_ref[...]
        m_new = jnp.maximum(m_old, jnp.max(s, axis=-1, keepdims=True))
        alpha = jnp.exp2(m_old - m_new)
        p = jnp.exp2(s - m_new)
        l_ref[...] = alpha * l_ref[...] + jnp.sum(p, axis=-1, keepdims=True)
        acc_ref[...] = alpha * acc_ref[...] + jnp.dot(p.astype(BF16), vb, preferred_element_type=F32)
        m_ref[...] = m_new

    step(kc_ref[0], vc_ref[0])
    if has_latent:
        def body(i, carry):
            off = pl.multiple_of(i * tk, tk)
            step(k_ref[0, pl.ds(off, tk), :], v_ref[0, pl.ds(off, tk), :])
            return carry
        lax.fori_loop(0, k_ref.shape[1] // tk, body, 0)

    dl = dl_ref[...]
    lam = (jnp.exp(jnp.sum(dl[0:1, :] * dl[1:2, :], axis=-1, keepdims=True))
           - jnp.exp(jnp.sum(dl[2:3, :] * dl[3:4, :], axis=-1, keepdims=True)) + lam_init)
    o = acc_ref[0:tq, :] / l_ref[0:tq, :] - lam * (acc_ref[tq:2 * tq, :] / l_ref[tq:2 * tq, :])
    o = o * lax.rsqrt(jnp.mean(o * o, axis=-1, keepdims=True) + RMS_EPS) * sg_ref[...]
    o_ref[0] = (o * (1.0 - lam_init)).astype(BF16)


def diff_attention(q, kc, vc, k, v, dlam, sg, lam_init, tq, tk):
    b, nq, _ = q.shape
    nc = kc.shape[1]
    has_latent = k is not None
    if not has_latent:
        k, v = kc, vc
    n = k.shape[1]
    assert nq % tq == 0 and (not has_latent or n % tk == 0)
    head_rows = lambda bb, h, i: (bb, i, h)
    head_all = lambda bb, h, i: (bb, 0, h)
    return pl.pallas_call(
        functools.partial(_attn_body, lam_init, tk, has_latent),
        grid=(b, N_HEADS, nq // tq),
        in_specs=[
            pl.BlockSpec((1, tq, V_DIM), head_rows),
            pl.BlockSpec((1, nc, V_DIM), head_all),
            pl.BlockSpec((1, nc, V_DIM), head_all),
            pl.BlockSpec((1, n, V_DIM), head_all),
            pl.BlockSpec((1, n, V_DIM), head_all),
            pl.BlockSpec((4, HEAD_DIM), lambda bb, h, i: (0, 0)),
            pl.BlockSpec((1, V_DIM), lambda bb, h, i: (0, 0)),
        ],
        out_specs=pl.BlockSpec((1, tq, V_DIM), head_rows),
        out_shape=jax.ShapeDtypeStruct((b, nq, ATT_W), BF16),
        scratch_shapes=[pltpu.VMEM((2 * tq, 1), F32), pltpu.VMEM((2 * tq, 1), F32),
                        pltpu.VMEM((2 * tq, V_DIM), F32)],
        compiler_params=_params(("arbitrary", "arbitrary", "arbitrary")),
        name="diff_attention" if has_latent else "diff_attention_ctx",
    )(q, kc, vc, k, v, dlam, sg)


def _s5_body(t_chunk, u_ref, wb_ref, a_ref, wc_ref, dsk_ref, y_ref, xs_ref, st_ref):
    d = pl.program_id(0)
    kk = pl.program_id(1)
    nb = u_ref.shape[1]

    @pl.when(kk == 0)
    def _():
        st_ref[...] = jnp.zeros(st_ref.shape, F32)

    u = u_ref[...].reshape(t_chunk * nb, S5_W)
    xs_ref[...] = jnp.dot(u.astype(BF16), wb_ref[0], preferred_element_type=F32).reshape(
        t_chunk, nb, 2 * S5_LANES)
    ar = jnp.broadcast_to(a_ref[0, 0:1, :], (nb, S5_LANES))
    ai = jnp.broadcast_to(a_ref[0, 1:2, :], (nb, S5_LANES))

    def body(i, carry):
        sr, si = carry
        t = jnp.where(d == 0, i, t_chunk - 1 - i)
        xr = xs_ref[t, :, 0:S5_LANES]
        xi = xs_ref[t, :, S5_LANES:2 * S5_LANES]
        nr = ar * sr - ai * si + xr
        ni = ar * si + ai * sr + xi
        xs_ref[t, :, 0:S5_LANES] = nr
        xs_ref[t, :, S5_LANES:2 * S5_LANES] = ni
        return nr, ni

    sr, si = lax.fori_loop(0, t_chunk, body, (st_ref[0], st_ref[1]), unroll=4)
    st_ref[0] = sr
    st_ref[1] = si
    y = jnp.dot(xs_ref[...].reshape(t_chunk * nb, 2 * S5_LANES).astype(BF16), wc_ref[0],
                preferred_element_type=F32)
    y = y + jnp.where(d == 0, 1.0, 0.0) * (dsk_ref[...] * u)
    y_ref[0] = y.reshape(t_chunk, nb, S5_W)


def s5_scan(u_all, wb, a, wc, dskip, n_ctx, t_chunk):
    tt, nb, _ = u_all.shape
    assert n_ctx % t_chunk == 0 and tt % t_chunk == 0
    nchunk = tt // t_chunk
    cctx = n_ctx // t_chunk

    def chunk_index(d, kk):
        bwd = jnp.where(kk < cctx, cctx - 1 - kk, nchunk - 1 - (kk - cctx))
        return jnp.where(d == 0, kk, bwd)

    return pl.pallas_call(
        functools.partial(_s5_body, t_chunk),
        grid=(2, nchunk),
        in_specs=[
            pl.BlockSpec((t_chunk, nb, S5_W), lambda d, kk: (chunk_index(d, kk), 0, 0)),
            pl.BlockSpec((1, S5_W, 2 * S5_LANES), lambda d, kk: (d, 0, 0)),
            pl.BlockSpec((1, 2, S5_LANES), lambda d, kk: (d, 0, 0)),
            pl.BlockSpec((1, 2 * S5_LANES, S5_W), lambda d, kk: (d, 0, 0)),
            pl.BlockSpec((1, S5_W), lambda d, kk: (0, 0)),
        ],
        out_specs=pl.BlockSpec((1, t_chunk, nb, S5_W), lambda d, kk: (d, chunk_index(d, kk), 0, 0)),
        out_shape=jax.ShapeDtypeStruct((2, tt, nb, S5_W), F32),
        scratch_shapes=[pltpu.VMEM((t_chunk, nb, 2 * S5_LANES), F32), pltpu.VMEM((2, nb, S5_LANES), F32)],
        compiler_params=_params(("arbitrary", "arbitrary")),
        name="s5_scan",
    )(u_all, wb, a, wc, dskip)


def s5_weights(lam_re, lam_im, log_dt, b_re, b_im, c_re, c_im):
    dt = jnp.exp(log_dt.astype(F32))[..., None]
    lr, li = lam_re.astype(F32), lam_im.astype(F32)
    mag = jnp.exp(lr * dt)
    ar, ai = mag * jnp.cos(li * dt), mag * jnp.sin(li * dt)
    den = lr * lr + li * li
    fr = ((ar - 1.0) * lr + ai * li) / den
    fi = (ai * lr - (ar - 1.0) * li) / den
    bbr = fr[..., None] * b_re - fi[..., None] * b_im
    bbi = fr[..., None] * b_im + fi[..., None] * b_re
    eye = jnp.eye(S5_G, dtype=F32)
    def in_map(bb):
        return jnp.einsum('dgnc,gh->dgchn', bb, eye).reshape(2, S5_W, S5_LANES)
    wb = jnp.concatenate([in_map(bbr), in_map(bbi)], axis=-1).astype(BF16)
    def out_map(cc):
        return jnp.einsum('dgcn,gh->dgnhc', cc, eye).reshape(2, S5_LANES, S5_W)
    wc = jnp.concatenate([out_map(c_re.astype(F32)), out_map(-c_im.astype(F32))], axis=1).astype(BF16)
    a = jnp.stack([ar.reshape(2, S5_LANES), ai.reshape(2, S5_LANES)], axis=1)
    return wb, a, wc


def _merge_body(x_ref, oatt_ref, yf_ref, yb_ref, ogm_ref, gate_ref, mod_ref, ng_ref, watt_ref, wglu_ref,
                wgm_ref, wout_ref, wr_ref, xo_ref, h2_ref, pr_ref):
    y_att = jnp.dot(oatt_ref[0], watt_ref[...], preferred_element_type=F32)
    y5 = yf_ref[0] + yb_ref[0]
    z = jnp.dot(jax.nn.gelu(y5).astype(BF16), wglu_ref[...], preferred_element_type=F32)
    y5g = z[:, :D_MODEL] * jax.nn.sigmoid(z[:, D_MODEL:])
    y_gm = jnp.dot(ogm_ref[0], wgm_ref[...], preferred_element_type=F32)
    merged = (gate_ref[0, :, 0:D_MODEL].astype(F32) * y_att
              + gate_ref[0, :, D_MODEL:2 * D_MODEL].astype(F32) * y5g
              + gate_ref[0, :, 2 * D_MODEL:3 * D_MODEL].astype(F32) * y_gm)
    mix = jnp.dot(merged.astype(BF16), wout_ref[...], preferred_element_type=F32)
    mixn = mix * lax.rsqrt(jnp.mean(mix * mix, axis=-1, keepdims=True) + RMS_EPS) * ng_ref[0:1, :]
    xn = x_ref[0] + mod_ref[0, 2:3, :] * mixn
    xo_ref[0] = xn
    h2 = xn * lax.rsqrt(jnp.mean(xn * xn, axis=-1, keepdims=True) + RMS_EPS) * ng_ref[1:2, :]
    h2 = h2 * (1.0 + mod_ref[0, 4:5, :]) + mod_ref[0, 3:4, :]
    h2_ref[0] = h2
    logits = lax.dot_general(wr_ref[...], h2.astype(BF16), (((1,), (1,)), ((), ())),
                             preferred_element_type=F32)
    pe = jnp.exp(logits - jnp.max(logits, axis=0, keepdims=True))
    pr_ref[0] = pe / jnp.sum(pe, axis=0, keepdims=True)


def merge_branches(x, o_att, yf, yb, ogm, gates, modv, ng12, watt, wglu, wgm, wout, wr_t, tm):
    b, n, d = x.shape
    row = lambda bb, j: (bb, j, 0)
    const2 = lambda bb, j: (0, 0)
    return pl.pallas_call(
        _merge_body,
        grid=(b, n // tm),
        in_specs=[
            pl.BlockSpec((1, tm, d), row),
            pl.BlockSpec((1, tm, ATT_W), row),
            pl.BlockSpec((1, tm, S5_W), row),
            pl.BlockSpec((1, tm, S5_W), row),
            pl.BlockSpec((1, tm, GM_W), row),
            pl.BlockSpec((1, tm, 3 * d), row),
            pl.BlockSpec((1, 6, d), lambda bb, j: (bb, 0, 0)),
            pl.BlockSpec((2, d), const2),
            pl.BlockSpec((ATT_W, d), const2),
            pl.BlockSpec((S5_W, 2 * d), const2),
            pl.BlockSpec((GM_W, d), const2),
            pl.BlockSpec((d, d), const2),
            pl.BlockSpec((N_EXP, d), const2),
        ],
        out_specs=[pl.BlockSpec((1, tm, d), row), pl.BlockSpec((1, tm, d), row),
                   pl.BlockSpec((1, N_EXP, tm), lambda bb, j: (bb, 0, j))],
        out_shape=[jax.ShapeDtypeStruct((b, n, d), F32), jax.ShapeDtypeStruct((b, n, d), F32),
                   jax.ShapeDtypeStruct((b, N_EXP, n), F32)],
        compiler_params=_params(("arbitrary", "arbitrary")),
        name="merge_branches",
    )(x, o_att, yf, yb, ogm, gates, modv, ng12, watt, wglu, wgm, wout, wr_t)


def _cumsum_lanes(x, tri):
    outs = []
    run = jnp.zeros((x.shape[0], 1), F32)
    for c in range(x.shape[1] // 128):
        blk = jnp.dot(x[:, c * 128:(c + 1) * 128].astype(BF16), tri, preferred_element_type=F32) + run
        outs.append(blk)
        run = blk[:, 127:128]
    return jnp.concatenate(outs, axis=1)


def _topk_body(cap, p_ref, idx_ref, cs_ref):
    p = p_ref[0]
    ne, n = p.shape
    bits = pltpu.bitcast(p, jnp.int32)
    thr = jnp.zeros((ne, 1), jnp.int32)
    for bit in range(30, -1, -1):
        cand = thr | (1 << bit)
        cnt = jnp.sum(jnp.where(bits >= cand, 1.0, 0.0), axis=-1, keepdims=True)
        thr = jnp.where(cnt >= cap, cand, thr)
    gt = bits > thr
    eq = bits == thr
    n_gt = jnp.sum(jnp.where(gt, 1.0, 0.0), axis=-1, keepdims=True)
    r = lax.broadcasted_iota(jnp.int32, (128, 128), 0)
    c = lax.broadcasted_iota(jnp.int32, (128, 128), 1)
    tri = jnp.where(r <= c, 1.0, 0.0).astype(BF16)
    eq_f = jnp.where(eq, 1.0, 0.0)
    eq_rank = _cumsum_lanes(eq_f, tri) - eq_f
    sel = jnp.where(gt | (eq & (eq_rank < cap - n_gt)), 1.0, 0.0)
    cs_ref[...] = _cumsum_lanes(sel, tri)
    nblk = idx_ref.shape[2] // 128

    def per_expert(e, carry):
        row = cs_ref[pl.ds(e, 1), :]
        blks = []
        for sb in range(nblk):
            slot = (lax.broadcasted_iota(jnp.int32, (128, 1), 0) + sb * 128).astype(F32)
            cnt = jnp.sum(jnp.where(row <= slot, 1.0, 0.0), axis=-1, keepdims=True)
            blks.append(jnp.broadcast_to(cnt, (128, 128)).T[0:1, :])
        idx_ref[0, pl.ds(e, 1), :] = jnp.concatenate(blks, axis=1).astype(jnp.int32)
        return carry
    lax.fori_loop(0, ne, per_expert, 0)


def expert_topk(probs_t, cap):
    b, ne, n = probs_t.shape
    cap_pad = -(-cap // 128) * 128
    return pl.pallas_call(
        functools.partial(_topk_body, cap),
        grid=(b,),
        in_specs=[pl.BlockSpec((1, ne, n), lambda bb: (bb, 0, 0))],
        out_specs=pl.BlockSpec((1, ne, cap_pad), lambda bb: (bb, 0, 0)),
        out_shape=jax.ShapeDtypeStruct((b, ne, cap_pad), jnp.int32),
        scratch_shapes=[pltpu.VMEM((ne, n), F32)],
        compiler_params=_params(("arbitrary",)),
        name="expert_topk",
    )(probs_t)


def _moe_body(cap, nb, idx_ref, idxn_ref, h2_hbm, wr_ref, wg_ref, wu_ref, wd_ref, ye_ref, xbuf, sem):
    e = pl.program_id(0)
    b = pl.program_id(1)
    step = e * nb + b
    nstep = pl.num_programs(0) * nb
    slot = step % 2

    def start_gather(idx_smem, bb, sl):
        def body(s, carry):
            t = idx_smem[0, 0, s]
            pltpu.make_async_copy(h2_hbm.at[bb, pl.ds(t, 1), :], xbuf.at[sl, pl.ds(s, 1), :], sem.at[sl]).start()
            return carry
        lax.fori_loop(0, cap, body, 0)

    @pl.when(step == 0)
    def _():
        start_gather(idx_ref, b, slot)

    @pl.when(step + 1 < nstep)
    def _():
        nxt = step + 1
        start_gather(idxn_ref, nxt % nb, 1 - slot)

    pltpu.make_async_copy(xbuf.at[slot], xbuf.at[slot], sem.at[slot]).wait()

    xe = xbuf[slot].astype(BF16)
    logits = jnp.dot(xe, wr_ref[...], preferred_element_type=F32)
    pe = jnp.exp(logits - jnp.max(logits, axis=-1, keepdims=True))
    probs = pe / jnp.sum(pe, axis=-1, keepdims=True)
    col = lax.broadcasted_iota(jnp.int32, probs.shape, 1)
    gate = jnp.sum(jnp.where(col == e, probs, 0.0), axis=-1, keepdims=True)
    ff = wg_ref.shape[2]
    ye = jnp.zeros((cap, xe.shape[1]), F32)
    for f0 in range(0, ff, FF_CHUNK):
        hg = jnp.dot(xe, wg_ref[0, :, f0:f0 + FF_CHUNK], preferred_element_type=F32)
        hu = jnp.dot(xe, wu_ref[0, :, f0:f0 + FF_CHUNK], preferred_element_type=F32)
        hid = (hg * jax.nn.sigmoid(hg) * hu).astype(BF16)
        ye = ye + jnp.dot(hid, wd_ref[0, f0:f0 + FF_CHUNK, :], preferred_element_type=F32)
    ye_ref[0, 0] = ye * gate


def moe_experts(h2, idx, cap, wr, wg, wu, wd):
    b, n, d = h2.shape
    ne, cap_pad = idx.shape[1:]
    ff = wg.shape[-1]
    idx3 = idx.reshape(b * ne, 1, cap_pad)

    def cur(e, bb):
        return (bb * ne + e, 0, 0)

    def nxt(e, bb):
        step = jnp.minimum(e * b + bb + 1, ne * b - 1)
        return ((step % b) * ne + step // b, 0, 0)

    return pl.pallas_call(
        functools.partial(_moe_body, cap, b),
        grid=(ne, b),
        in_specs=[
            pl.BlockSpec((1, 1, cap_pad), cur, memory_space=pltpu.SMEM),
            pl.BlockSpec((1, 1, cap_pad), nxt, memory_space=pltpu.SMEM),
            pl.BlockSpec(memory_space=pl.ANY),
            pl.BlockSpec((d, ne), lambda e, bb: (0, 0)),
            pl.BlockSpec((1, d, ff), lambda e, bb: (e, 0, 0)),
            pl.BlockSpec((1, d, ff), lambda e, bb: (e, 0, 0)),
            pl.BlockSpec((1, ff, d), lambda e, bb: (e, 0, 0)),
        ],
        out_specs=pl.BlockSpec((1, 1, cap, d), lambda e, bb: (bb, e, 0, 0)),
        out_shape=jax.ShapeDtypeStruct((b, ne, cap, d), F32),
        scratch_shapes=[pltpu.VMEM((2, cap, d), F32), pltpu.SemaphoreType.DMA((2,))],
        compiler_params=_params(("arbitrary", "arbitrary")),
        name="moe_experts",
    )(idx3, idx3, h2, wr, wg, wu, wd)


def _combine_body(cap, idx_ref, ye_ref, acc_ref):
    e = pl.program_id(2)

    @pl.when(e == 0)
    def _():
        acc_ref[...] = jnp.zeros(acc_ref.shape, F32)

    def body(s, carry):
        t = idx_ref[0, 0, s]
        acc_ref[0, pl.ds(t, 1), :] = acc_ref[0, pl.ds(t, 1), :] + ye_ref[0, 0, pl.ds(s, 1), :]
        return carry
    lax.fori_loop(0, cap, body, 0, unroll=8 if cap % 8 == 0 else 1)


def moe_combine(ye, idx, n, dh=512):
    b, ne, cap, d = ye.shape
    cap_pad = idx.shape[2]
    idx3 = idx.reshape(b * ne, 1, cap_pad)
    return pl.pallas_call(
        functools.partial(_combine_body, cap),
        grid=(b, d // dh, ne),
        in_specs=[
            pl.BlockSpec((1, 1, cap_pad), lambda bb, j, e: (bb * ne + e, 0, 0), memory_space=pltpu.SMEM),
            pl.BlockSpec((1, 1, cap, dh), lambda bb, j, e: (bb, e, 0, j)),
        ],
        out_specs=pl.BlockSpec((1, n, dh), lambda bb, j, e: (bb, 0, j)),
        out_shape=jax.ShapeDtypeStruct((b, n, d), F32),
        compiler_params=_params(("arbitrary", "arbitrary", "arbitrary")),
        name="moe_combine",
    )(idx3, ye)


def _resid_body(x_ref, y_ref, mod_ref, ng_ref, o_ref):
    y = y_ref[0]
    yn = y * lax.rsqrt(jnp.mean(y * y, axis=-1, keepdims=True) + RMS_EPS) * ng_ref[...]
    o_ref[0] = x_ref[0] + mod_ref[0, 5:6, :] * yn


def moe_residual(x, y, modv, ng3, tm):
    b, n, d = x.shape
    row = lambda bb, j: (bb, j, 0)
    return pl.pallas_call(
        _resid_body,
        grid=(b, n // tm),
        in_specs=[pl.BlockSpec((1, tm, d), row), pl.BlockSpec((1, tm, d), row),
                  pl.BlockSpec((1, 6, d), lambda bb, j: (bb, 0, 0)), pl.BlockSpec((1, d), lambda bb, j: (0, 0))],
        out_specs=pl.BlockSpec((1, tm, d), row),
        out_shape=jax.ShapeDtypeStruct((b, n, d), F32),
        compiler_params=_params(("arbitrary", "arbitrary")),
        name="moe_residual",
    )(x, y, modv, ng3)


def _rope_tables(n):
    rows = n // GRID_W
    row = jnp.repeat(jnp.arange(rows), GRID_W).astype(F32)
    col = jnp.tile(jnp.arange(GRID_W), rows).astype(F32)
    inv = ROPE_THETA ** (-jnp.arange(ROPE_FREQS, dtype=F32) / ROPE_FREQS)
    ar, ac = row[:, None] * inv, col[:, None] * inv
    cos64 = jnp.concatenate([jnp.cos(ar), jnp.cos(ar), jnp.cos(ac), jnp.cos(ac)], axis=1)
    sin64 = jnp.concatenate([-jnp.sin(ar), jnp.sin(ar), -jnp.sin(ac), jnp.sin(ac)], axis=1)
    return jnp.tile(cos64, (1, 2)), jnp.tile(sin64, (1, 2))


def _moe_sublayer(x_mid, h2, probs_t, modv, ng3, wr, wg, wu, wd, tm):
    n = x_mid.shape[1]
    cap = CAP_FACTOR * n // N_EXP
    idx = expert_topk(probs_t, cap)
    ye = moe_experts(h2, idx, cap, wr, wg, wu, wd)
    y = moe_combine(ye, idx, n)
    return moe_residual(x_mid, y, modv, ng3, tm)


def _forward(x, c, ctx, c_ctx, w_mod, b_mod, norm_g, w_in, diff_lambda, diff_subln_g, s5_lam_re, s5_lam_im,
             s5_log_dt, s5_b_re, s5_b_im, s5_c_re, s5_c_im, s5_d, w_s5_glu, gmlp_norm_g, gmlp_ws, gmlp_bs,
             w_br_att, w_br_gmlp, w_out, w_router, w_e_gate, w_e_up, w_e_down, *, tm, tq, tk, t_chunk):
    b, n, d = x.shape
    n_ctx = ctx.shape[1]
    depth = w_mod.shape[0]
    tm_c = min(tm, n_ctx)
    tq_c = min(tq, n_ctx)

    c_rows = jnp.zeros((MOD_ROWS, d), F32).at[:b].set(c).at[b].set(c_ctx)
    mod_all = modulation(c_rows, w_mod, b_mod).reshape(depth, MOD_ROWS, 6, d)
    cos_t, sin_t = _rope_tables(n)
    cos_c, sin_c = jnp.ones((n_ctx, 128), F32), jnp.zeros((n_ctx, 128), F32)

    xc = ctx
    for l in range(depth):
        last = l == depth - 1
        lam_init = 0.8 - 0.6 * math.exp(-0.3 * l)
        modv = mod_all[l, :b]
        modc = jnp.broadcast_to(mod_all[l, b:b + 1], (b, 6, d))
        ng0 = norm_g[l, 0:1]
        w_in_b = w_in[l].astype(BF16)
        gg = gmlp_norm_g[l][None, :]
        ws = gmlp_ws[l].astype(BF16)
        bs = jnp.repeat(gmlp_bs[l].T, GM_HD, axis=1)
        sg = diff_subln_g[l][None, :]
        wr = w_router[l].astype(BF16)
        wg, wu, wd = w_e_gate[l].astype(BF16), w_e_up[l].astype(BF16), w_e_down[l].astype(BF16)

        q, k, v, u, ogm, gates = input_projection(x, modv, ng0, w_in_b, cos_t, sin_t, gg, ws, bs, True, tm)
        pc = input_projection(xc, modc, ng0, w_in_b, cos_c, sin_c, gg, ws, bs, not last, tm_c)
        if last:
            kc, vc, uc = pc
        else:
            qc, kc, vc, uc, ogm_c, gates_c = pc

        o_att = diff_attention(q, kc, vc, k, v, diff_lambda[l], sg, lam_init, tq, tk)

        wb, a, wc = s5_weights(s5_lam_re[l], s5_lam_im[l], s5_log_dt[l], s5_b_re[l], s5_b_im[l],
                               s5_c_re[l], s5_c_im[l])
        u_all = jnp.concatenate([uc, u], axis=1).transpose(1, 0, 2)
        y5 = s5_scan(u_all, wb, a, wc, s5_d[l][None, :], n_ctx, min(t_chunk, n_ctx)).transpose(0, 2, 1, 3)

        merge_w = (norm_g[l, 1:3], w_br_att[l].astype(BF16), w_s5_glu[l].astype(BF16),
                   w_br_gmlp[l].astype(BF16), w_out[l].astype(BF16), w_router[l].T.astype(BF16))
        x_mid, h2, probs_t = merge_branches(x, o_att, y5[0, :, n_ctx:], y5[1, :, n_ctx:], ogm, gates, modv,
                                            *merge_w, tm)
        x = _moe_sublayer(x_mid, h2, probs_t, modv, norm_g[l, 3:4], wr, wg, wu, wd, tm)
        if not last:
            o_att_c = diff_attention(qc, kc, vc, None, None, diff_lambda[l], sg, lam_init, tq_c, tk)
            xc_mid, h2c, probs_c = merge_branches(xc, o_att_c, y5[0, :, :n_ctx], y5[1, :, :n_ctx], ogm_c,
                                                  gates_c, modc, *merge_w, tm_c)
            xc = _moe_sublayer(xc_mid, h2c, probs_c, modc, norm_g[l, 3:4], wr, wg, wu, wd, tm_c)
    return x


def kernel(x, c, ctx, c_ctx, w_mod, b_mod, norm_g, w_in, diff_lambda, diff_subln_g, s5_lam_re, s5_lam_im, s5_log_dt, s5_b_re, s5_b_im, s5_c_re, s5_c_im, s5_d, w_s5_glu, gmlp_norm_g, gmlp_ws, gmlp_bs, w_br_att, w_br_gmlp, w_out, w_router, w_e_gate, w_e_up, w_e_down):
    return _forward(x, c, ctx, c_ctx, w_mod, b_mod, norm_g, w_in, diff_lambda, diff_subln_g, s5_lam_re, s5_lam_im,
                    s5_log_dt, s5_b_re, s5_b_im, s5_c_re, s5_c_im, s5_d, w_s5_glu, gmlp_norm_g, gmlp_ws, gmlp_bs,
                    w_br_att, w_br_gmlp, w_out, w_router, w_e_gate, w_e_up, w_e_down,
                    tm=512, tq=512, tk=512, t_chunk=256)
```

```python
import functools
import math

import jax
import jax.numpy as jnp
from jax import lax
from jax.experimental import pallas as pl
from jax.experimental.pallas import tpu as pltpu

F32 = jnp.float32
BF16 = jnp.bfloat16

D_MODEL = 1024
GRID_W = 64
RMS_EPS = 1e-6
N_HEADS = 4
HEAD_DIM = 64
V_DIM = 2 * HEAD_DIM
ATT_W = N_HEADS * V_DIM
ROPE_THETA = 10000.0
ROPE_FREQS = HEAD_DIM // 4
S5_W = D_MODEL // 4
S5_P = 16
S5_G = S5_W // S5_P
S5_N = 64
S5_LANES = S5_G * S5_N
GM_W = D_MODEL // 4
GM_H = 4
GM_HD = GM_W // GM_H
CHUNK = 128
N_EXP = 16
CAP_FACTOR = 2
K_END = ATT_W
V_END = K_END + ATT_W
S5_END = V_END + S5_W
Q_END = S5_END + ATT_W
GMLP_END = Q_END + 2 * GM_W
IN_COLS = GMLP_END + 3 * D_MODEL
LOG2E = 1.4426950408889634

V7X_VMEM_BYTES = 64 * 1024 * 1024
VMEM_LIMIT = 56 * 1024 * 1024
FF_CHUNK = 512
ATT_STRIP = 256
MOD_ROWS = 16


def _params(sem):
    return pltpu.CompilerParams(dimension_semantics=sem, vmem_limit_bytes=VMEM_LIMIT)


def _mod_body(c_ref, w_ref, b_ref, o_ref):
    c = c_ref[...]
    s = (c * jax.nn.sigmoid(c)).astype(BF16)
    o_ref[0] = jnp.dot(s, w_ref[0].astype(BF16), preferred_element_type=F32) + b_ref[0]


def modulation(c_rows, w_mod, b_mod, tn=1536):
    depth, d, n6 = w_mod.shape
    return pl.pallas_call(
        _mod_body,
        grid=(depth, n6 // tn),
        in_specs=[
            pl.BlockSpec((MOD_ROWS, d), lambda l, j: (0, 0)),
            pl.BlockSpec((1, d, tn), lambda l, j: (l, 0, j)),
            pl.BlockSpec((1, 1, tn), lambda l, j: (l, 0, j)),
        ],
        out_specs=pl.BlockSpec((1, MOD_ROWS, tn), lambda l, j: (l, 0, j)),
        out_shape=jax.ShapeDtypeStruct((depth, MOD_ROWS, n6), F32),
        compiler_params=_params(("arbitrary", "arbitrary")),
        name="modulation",
    )(c_rows, w_mod, b_mod.reshape(depth, 1, n6))


def _rope_cols(p, cos, sin):
    lane = lax.broadcasted_iota(jnp.int32, (p.shape[0], 128), 1)
    first = (lane % 32) < ROPE_FREQS
    outs = []
    for s in range(p.shape[1] // 128):
        ps = p[:, s * 128:(s + 1) * 128]
        sw = jnp.where(first, pltpu.roll(ps, 128 - ROPE_FREQS, 1), pltpu.roll(ps, ROPE_FREQS, 1))
        outs.append(ps * cos + sw * sin)
    return jnp.concatenate(outs, axis=1)


def _inproj_body(full, x_ref, mod_ref, ng_ref, w_ref, cos_ref, sin_ref, gg_ref, ws_ref, bs_ref, *outs):
    xt = x_ref[0]
    h = xt * lax.rsqrt(jnp.mean(xt * xt, axis=-1, keepdims=True) + RMS_EPS) * ng_ref[...]
    h = h * (1.0 + mod_ref[0, 1:2, :]) + mod_ref[0, 0:1, :]
    hb = h.astype(BF16)

    def proj(a, b):
        return jnp.dot(hb, w_ref[:, a:b], preferred_element_type=F32)

    cos = cos_ref[...]
    sin = sin_ref[...]
    if full:
        q_ref, k_ref, v_ref, u_ref, ogm_ref, gate_ref = outs
    else:
        k_ref, v_ref, u_ref = outs
    k_ref[0] = _rope_cols(proj(0, K_END), cos, sin).astype(BF16)
    v_ref[0] = proj(K_END, V_END).T.astype(BF16)
    u_ref[0] = proj(V_END, S5_END)
    if not full:
        return
    q_ref[0] = (_rope_cols(proj(S5_END, Q_END), cos, sin) * (HEAD_DIM ** -0.5 * LOG2E)).astype(BF16)

    zg = jax.nn.gelu(proj(Q_END, GMLP_END))
    zu = zg[:, :GM_W]
    zv = zg[:, GM_W:]
    mu = jnp.mean(zv, axis=-1, keepdims=True)
    zc = zv - mu
    zvn = (zc * lax.rsqrt(jnp.mean(zc * zc, axis=-1, keepdims=True) + RMS_EPS) * gg_ref[...]).astype(BF16)
    head = lax.broadcasted_iota(jnp.int32, (CHUNK, GM_W), 1) // GM_HD
    tm = xt.shape[0]
    for c in range(tm // CHUNK):
        zc_blk = zvn[c * CHUNK:(c + 1) * CHUNK, :]
        mixed = bs_ref[...]
        for g in range(GM_H):
            mixed = mixed + jnp.dot(ws_ref[g], jnp.where(head == g, zc_blk, jnp.zeros_like(zc_blk)),
                                    preferred_element_type=F32)
        ogm_ref[0, c * CHUNK:(c + 1) * CHUNK, :] = (zu[c * CHUNK:(c + 1) * CHUNK, :] * mixed).astype(BF16)

    for j in range(3):
        a = GMLP_END + j * D_MODEL
        gate_ref[0, :, j * D_MODEL:(j + 1) * D_MODEL] = jax.nn.sigmoid(proj(a, a + D_MODEL)).astype(BF16)


def input_projection(x, modv, ng, w_in, cos_t, sin_t, gg, ws, bs, full, tm):
    b, n, d = x.shape
    assert n % tm == 0 and tm % CHUNK == 0
    row = lambda bb, j: (bb, j, 0)
    const2 = lambda bb, j: (0, 0)
    out_shape = [jax.ShapeDtypeStruct((b, n, ATT_W), BF16),
                 jax.ShapeDtypeStruct((b, ATT_W, n), BF16),
                 jax.ShapeDtypeStruct((b, n, S5_W), F32)]
    out_specs = [pl.BlockSpec((1, tm, ATT_W), row), pl.BlockSpec((1, ATT_W, tm), lambda bb, j: (bb, 0, j)),
                 pl.BlockSpec((1, tm, S5_W), row)]
    if full:
        out_shape = [jax.ShapeDtypeStruct((b, n, ATT_W), BF16)] + out_shape + [
            jax.ShapeDtypeStruct((b, n, GM_W), BF16), jax.ShapeDtypeStruct((b, n, 3 * d), BF16)]
        out_specs = [pl.BlockSpec((1, tm, ATT_W), row)] + out_specs + [
            pl.BlockSpec((1, tm, GM_W), row), pl.BlockSpec((1, tm, 3 * d), row)]
    return pl.pallas_call(
        functools.partial(_inproj_body, full),
        grid=(b, n // tm),
        in_specs=[
            pl.BlockSpec((1, tm, d), row),
            pl.BlockSpec((1, 6, d), lambda bb, j: (bb, 0, 0)),
            pl.BlockSpec((1, d), const2),
            pl.BlockSpec((d, IN_COLS), const2),
            pl.BlockSpec((tm, 128), lambda bb, j: (j, 0)),
            pl.BlockSpec((tm, 128), lambda bb, j: (j, 0)),
            pl.BlockSpec((1, GM_W), const2),
            pl.BlockSpec((GM_H, CHUNK, CHUNK), lambda bb, j: (0, 0, 0)),
            pl.BlockSpec((CHUNK, GM_W), const2),
        ],
        out_specs=out_specs,
        out_shape=out_shape,
        compiler_params=_params(("arbitrary", "arbitrary")),
        name="input_projection" if full else "input_projection_ctx_side",
    )(x, modv, ng, w_in, cos_t, sin_t, gg, ws, bs)


def _attn_body(lam_init, tk, has_latent, q_ref, kc_ref, vtc_ref, k_ref, vt_ref, dl_ref, sg_ref, o_ref,
               m_ref, l_ref, acc_ref, q2_ref, s_ref, p_ref, al_ref):
    tq = q_ref.shape[1]
    q = q_ref[0]
    lane = lax.broadcasted_iota(jnp.int32, q.shape, 1)
    zero = jnp.zeros_like(q)
    q2_ref[0:tq, :] = jnp.where(lane < HEAD_DIM, q, zero)
    q2_ref[tq:2 * tq, :] = jnp.where(lane >= HEAD_DIM, q, zero)
    m_ref[...] = jnp.full(m_ref.shape, -jnp.inf, F32)
    l_ref[...] = jnp.zeros(l_ref.shape, F32)
    acc_ref[...] = jnp.zeros(acc_ref.shape, F32)

    def scores(kb):
        return lax.dot_general(kb, q2_ref[...], (((1,), (1,)), ((), ())), preferred_element_type=F32)

    def softmax_strip(st, cols):
        m_old = m_ref[:, cols]
        m_new = jnp.maximum(m_old, jnp.max(st, axis=0, keepdims=True))
        p = jnp.exp2(st - m_new)
        alpha = jnp.exp2(m_old - m_new)
        l_ref[:, cols] = alpha * l_ref[:, cols] + jnp.sum(p, axis=0, keepdims=True)
        m_ref[:, cols] = m_new
        return p.astype(BF16), alpha

    def accumulate(vtb, p, alpha):
        acc_ref[...] = alpha * acc_ref[...] + jnp.dot(vtb, p, preferred_element_type=F32)

    st_c = scores(kc_ref[0])
    parts = [softmax_strip(st_c[:, c * 128:(c + 1) * 128], slice(c * 128, (c + 1) * 128))
             for c in range(2 * tq // 128)]
    accumulate(vtc_ref[0], jnp.concatenate([pp for pp, _ in parts], axis=1),
               jnp.concatenate([aa for _, aa in parts], axis=1))

    if has_latent:
        n_kv = k_ref.shape[1] // tk
        assert n_kv >= 4 and n_kv % 2 == 0

        def qk(j, slot):
            off = pl.multiple_of(j * tk, tk)
            s_ref[slot] = scores(k_ref[0, pl.ds(off, tk), :])

        def sm(slot):
            for c in range(2 * tq // 128):
                cols = slice(c * 128, (c + 1) * 128)
                pp, aa = softmax_strip(s_ref[slot, :, cols], cols)
                p_ref[slot, :, cols] = pp
                al_ref[slot, :, cols] = aa

        def pv(j, slot):
            off = pl.multiple_of(j * tk, tk)
            accumulate(vt_ref[0, :, pl.ds(off, tk)], p_ref[slot], al_ref[slot])

        qk(0, 0)
        sm(0)
        qk(1, 1)

        def body(i, carry):
            t = 2 * i + 1
            pv(t - 1, 0)
            qk(t + 1, 0)
            sm(1)
            pv(t, 1)
            qk(t + 2, 1)
            sm(0)
            return carry
        lax.fori_loop(0, (n_kv - 2) // 2, body, 0)
        pv(n_kv - 2, 0)
        sm(1)
        pv(n_kv - 1, 1)

    dl = dl_ref[...]
    lam = (jnp.exp(jnp.sum(dl[0:1, :] * dl[1:2, :], axis=-1, keepdims=True))
           - jnp.exp(jnp.sum(dl[2:3, :] * dl[3:4, :], axis=-1, keepdims=True)) + lam_init)
    ot = acc_ref[:, 0:tq] / l_ref[:, 0:tq] - lam * (acc_ref[:, tq:2 * tq] / l_ref[:, tq:2 * tq])
    ot = ot * lax.rsqrt(jnp.mean(ot * ot, axis=0, keepdims=True) + RMS_EPS)
    o_ref[0] = (ot.T * sg_ref[...] * (1.0 - lam_init)).astype(BF16)


def diff_attention(q, kc, vtc, k, vt, dlam, sg, lam_init, tq, tk):
    b, nq, _ = q.shape
    nc = kc.shape[1]
    has_latent = k is not None
    if not has_latent:
        k, vt = kc, vtc
    n = k.shape[1]
    assert nq % tq == 0 and (not has_latent or n % tk == 0)
    head_rows = lambda bb, h, i: (bb, i, h)
    head_all = lambda bb, h, i: (bb, 0, h)
    head_all_t = lambda bb, h, i: (bb, h, 0)
    return pl.pallas_call(
        functools.partial(_attn_body, lam_init, tk, has_latent),
        grid=(b, N_HEADS, nq // tq),
        in_specs=[
            pl.BlockSpec((1, tq, V_DIM), head_rows),
            pl.BlockSpec((1, nc, V_DIM), head_all),
            pl.BlockSpec((1, V_DIM, nc), head_all_t),
            pl.BlockSpec((1, n, V_DIM), head_all),
            pl.BlockSpec((1, V_DIM, n), head_all_t),
            pl.BlockSpec((4, HEAD_DIM), lambda bb, h, i: (0, 0)),
            pl.BlockSpec((1, V_DIM), lambda bb, h, i: (0, 0)),
        ],
        out_specs=pl.BlockSpec((1, tq, V_DIM), head_rows),
        out_shape=jax.ShapeDtypeStruct((b, nq, ATT_W), BF16),
        scratch_shapes=[pltpu.VMEM((1, 2 * tq), F32), pltpu.VMEM((1, 2 * tq), F32),
                        pltpu.VMEM((V_DIM, 2 * tq), F32), pltpu.VMEM((2 * tq, V_DIM), BF16),
                        pltpu.VMEM((2, tk, 2 * tq), F32), pltpu.VMEM((2, tk, 2 * tq), BF16),
                        pltpu.VMEM((2, 1, 2 * tq), F32)],
        compiler_params=_params(("arbitrary", "arbitrary", "arbitrary")),
        name="diff_attention" if has_latent else "diff_attention_ctx",
    )(q, kc, vtc, k, vt, dlam, sg)


def _s5_body(t_chunk, u_ref, wb_ref, a_ref, wc_ref, dsk_ref, y_ref, xs_ref, st_ref):
    d = pl.program_id(0)
    kk = pl.program_id(1)
    nb = u_ref.shape[1]

    @pl.when(kk == 0)
    def _():
        st_ref[...] = jnp.zeros(st_ref.shape, F32)

    u = u_ref[...].reshape(t_chunk * nb, S5_W)
    xs_ref[...] = jnp.dot(u.astype(BF16), wb_ref[0], preferred_element_type=F32).reshape(
        t_chunk, nb, 2 * S5_LANES)
    ar = jnp.broadcast_to(a_ref[0, 0:1, :], (nb, S5_LANES))
    ai = jnp.broadcast_to(a_ref[0, 1:2, :], (nb, S5_LANES))

    def body(i, carry):
        sr, si = carry
        t = jnp.where(d == 0, i, t_chunk - 1 - i)
        xr = xs_ref[t, :, 0:S5_LANES]
        xi = xs_ref[t, :, S5_LANES:2 * S5_LANES]
        nr = ar * sr - ai * si + xr
        ni = ar * si + ai * sr + xi
        xs_ref[t, :, 0:S5_LANES] = nr
        xs_ref[t, :, S5_LANES:2 * S5_LANES] = ni
        return nr, ni

    sr, si = lax.fori_loop(0, t_chunk, body, (st_ref[0], st_ref[1]), unroll=4)
    st_ref[0] = sr
    st_ref[1] = si
    y = jnp.dot(xs_ref[...].reshape(t_chunk * nb, 2 * S5_LANES).astype(BF16), wc_ref[0],
                preferred_element_type=F32)
    y = y + jnp.where(d == 0, 1.0, 0.0) * (dsk_ref[...] * u)
    y_ref[0] = y.reshape(t_chunk, nb, S5_W)


def s5_scan(u_all, wb, a, wc, dskip, n_ctx, t_chunk):
    tt, nb, _ = u_all.shape
    assert n_ctx % t_chunk == 0 and tt % t_chunk == 0
    nchunk = tt // t_chunk
    cctx = n_ctx // t_chunk

    def chunk_index(d, kk):
        bwd = jnp.where(kk < cctx, cctx - 1 - kk, nchunk - 1 - (kk - cctx))
        return jnp.where(d == 0, kk, bwd)

    return pl.pallas_call(
        functools.partial(_s5_body, t_chunk),
        grid=(2, nchunk),
        in_specs=[
            pl.BlockSpec((t_chunk, nb, S5_W), lambda d, kk: (chunk_index(d, kk), 0, 0)),
            pl.BlockSpec((1, S5_W, 2 * S5_LANES), lambda d, kk: (d, 0, 0)),
            pl.BlockSpec((1, 2, S5_LANES), lambda d, kk: (d, 0, 0)),
            pl.BlockSpec((1, 2 * S5_LANES, S5_W), lambda d, kk: (d, 0, 0)),
            pl.BlockSpec((1, S5_W), lambda d, kk: (0, 0)),
        ],
        out_specs=pl.BlockSpec((1, t_chunk, nb, S5_W), lambda d, kk: (d, chunk_index(d, kk), 0, 0)),
        out_shape=jax.ShapeDtypeStruct((2, tt, nb, S5_W), F32),
        scratch_shapes=[pltpu.VMEM((t_chunk, nb, 2 * S5_LANES), F32), pltpu.VMEM((2, nb, S5_LANES), F32)],
        compiler_params=_params(("arbitrary", "arbitrary")),
        name="s5_scan",
    )(u_all, wb, a, wc, dskip)


def s5_weights(lam_re, lam_im, log_dt, b_re, b_im, c_re, c_im):
    dt = jnp.exp(log_dt.astype(F32))[..., None]
    lr, li = lam_re.astype(F32), lam_im.astype(F32)
    mag = jnp.exp(lr * dt)
    ar, ai = mag * jnp.cos(li * dt), mag * jnp.sin(li * dt)
    den = lr * lr + li * li
    fr = ((ar - 1.0) * lr + ai * li) / den
    fi = (ai * lr - (ar - 1.0) * li) / den
    bbr = fr[..., None] * b_re - fi[..., None] * b_im
    bbi = fr[..., None] * b_im + fi[..., None] * b_re
    eye = jnp.eye(S5_G, dtype=F32)
    def in_map(bb):
        return jnp.einsum('dgnc,gh->dgchn', bb, eye).reshape(2, S5_W, S5_LANES)
    wb = jnp.concatenate([in_map(bbr), in_map(bbi)], axis=-1).astype(BF16)
    def out_map(cc):
        return jnp.einsum('dgcn,gh->dgnhc', cc, eye).reshape(2, S5_LANES, S5_W)
    wc = jnp.concatenate([out_map(c_re.astype(F32)), out_map(-c_im.astype(F32))], axis=1).astype(BF16)
    a = jnp.stack([ar.reshape(2, S5_LANES), ai.reshape(2, S5_LANES)], axis=1)
    return wb, a, wc


def _merge_body(x_ref, oatt_ref, yf_ref, yb_ref, ogm_ref, gate_ref, mod_ref, ng_ref, watt_ref, wglu_ref,
                wgm_ref, wout_ref, wr_ref, xo_ref, h2_ref, pr_ref):
    y_att = jnp.dot(oatt_ref[0], watt_ref[...], preferred_element_type=F32)
    y5 = yf_ref[0] + yb_ref[0]
    z = jnp.dot(jax.nn.gelu(y5).astype(BF16), wglu_ref[...], preferred_element_type=F32)
    y5g = z[:, :D_MODEL] * jax.nn.sigmoid(z[:, D_MODEL:])
    y_gm = jnp.dot(ogm_ref[0], wgm_ref[...], preferred_element_type=F32)
    merged = (gate_ref[0, :, 0:D_MODEL].astype(F32) * y_att
              + gate_ref[0, :, D_MODEL:2 * D_MODEL].astype(F32) * y5g
              + gate_ref[0, :, 2 * D_MODEL:3 * D_MODEL].astype(F32) * y_gm)
    mix = jnp.dot(merged.astype(BF16), wout_ref[...], preferred_element_type=F32)
    mixn = mix * lax.rsqrt(jnp.mean(mix * mix, axis=-1, keepdims=True) + RMS_EPS) * ng_ref[0:1, :]
    xn = x_ref[0] + mod_ref[0, 2:3, :] * mixn
    xo_ref[0] = xn
    h2 = xn * lax.rsqrt(jnp.mean(xn * xn, axis=-1, keepdims=True) + RMS_EPS) * ng_ref[1:2, :]
    h2 = h2 * (1.0 + mod_ref[0, 4:5, :]) + mod_ref[0, 3:4, :]
    h2_ref[0] = h2
    logits = lax.dot_general(wr_ref[...], h2.astype(BF16), (((1,), (1,)), ((), ())),
                             preferred_element_type=F32)
    pe = jnp.exp(logits - jnp.max(logits, axis=0, keepdims=True))
    pr_ref[0] = pe / jnp.sum(pe, axis=0, keepdims=True)


def merge_branches(x, o_att, yf, yb, ogm, gates, modv, ng12, watt, wglu, wgm, wout, wr_t, tm):
    b, n, d = x.shape
    row = lambda bb, j: (bb, j, 0)
    const2 = lambda bb, j: (0, 0)
    return pl.pallas_call(
        _merge_body,
        grid=(b, n // tm),
        in_specs=[
            pl.BlockSpec((1, tm, d), row),
            pl.BlockSpec((1, tm, ATT_W), row),
            pl.BlockSpec((1, tm, S5_W), row),
            pl.BlockSpec((1, tm, S5_W), row),
            pl.BlockSpec((1, tm, GM_W), row),
            pl.BlockSpec((1, tm, 3 * d), row),
            pl.BlockSpec((1, 6, d), lambda bb, j: (bb, 0, 0)),
            pl.BlockSpec((2, d), const2),
            pl.BlockSpec((ATT_W, d), const2),
            pl.BlockSpec((S5_W, 2 * d), const2),
            pl.BlockSpec((GM_W, d), const2),
            pl.BlockSpec((d, d), const2),
            pl.BlockSpec((N_EXP, d), const2),
        ],
        out_specs=[pl.BlockSpec((1, tm, d), row), pl.BlockSpec((1, tm, d), row),
                   pl.BlockSpec((1, N_EXP, tm), lambda bb, j: (bb, 0, j))],
        out_shape=[jax.ShapeDtypeStruct((b, n, d), F32), jax.ShapeDtypeStruct((b, n, d), F32),
                   jax.ShapeDtypeStruct((b, N_EXP, n), F32)],
        compiler_params=_params(("arbitrary", "arbitrary")),
        name="merge_branches",
    )(x, o_att, yf, yb, ogm, gates, modv, ng12, watt, wglu, wgm, wout, wr_t)


def _cumsum_lanes(x, tri):
    outs = []
    run = jnp.zeros((x.shape[0], 1), F32)
    for c in range(x.shape[1] // 128):
        blk = jnp.dot(x[:, c * 128:(c + 1) * 128].astype(BF16), tri, preferred_element_type=F32) + run
        outs.append(blk)
        run = blk[:, 127:128]
    return jnp.concatenate(outs, axis=1)


def _topk_body(cap, p_ref, idx_ref, cs_ref):
    p = p_ref[0]
    ne, n = p.shape
    bits = pltpu.bitcast(p, jnp.int32)
    thr = jnp.zeros((ne, 1), jnp.int32)
    for bit in range(30, -1, -1):
        cand = thr | (1 << bit)
        cnt = jnp.sum(jnp.where(bits >= cand, 1.0, 0.0), axis=-1, keepdims=True)
        thr = jnp.where(cnt >= cap, cand, thr)
    gt = bits > thr
    eq = bits == thr
    n_gt = jnp.sum(jnp.where(gt, 1.0, 0.0), axis=-1, keepdims=True)
    r = lax.broadcasted_iota(jnp.int32, (128, 128), 0)
    c = lax.broadcasted_iota(jnp.int32, (128, 128), 1)
    tri = jnp.where(r <= c, 1.0, 0.0).astype(BF16)
    eq_f = jnp.where(eq, 1.0, 0.0)
    eq_rank = _cumsum_lanes(eq_f, tri) - eq_f
    sel = jnp.where(gt | (eq & (eq_rank < cap - n_gt)), 1.0, 0.0)
    cs_ref[...] = _cumsum_lanes(sel, tri)
    nblk = idx_ref.shape[2] // 128

    def per_expert(e, carry):
        row = cs_ref[pl.ds(e, 1), :]
        blks = []
        for sb in range(nblk):
            slot = (lax.broadcasted_iota(jnp.int32, (128, 1), 0) + sb * 128).astype(F32)
            cnt = jnp.sum(jnp.where(row <= slot, 1.0, 0.0), axis=-1, keepdims=True)
            blks.append(jnp.broadcast_to(cnt, (128, 128)).T[0:1, :])
        idx_ref[0, pl.ds(e, 1), :] = jnp.concatenate(blks, axis=1).astype(jnp.int32)
        return carry
    lax.fori_loop(0, ne, per_expert, 0)


def expert_topk(probs_t, cap):
    b, ne, n = probs_t.shape
    cap_pad = -(-cap // 128) * 128
    return pl.pallas_call(
        functools.partial(_topk_body, cap),
        grid=(b,),
        in_specs=[pl.BlockSpec((1, ne, n), lambda bb: (bb, 0, 0))],
        out_specs=pl.BlockSpec((1, ne, cap_pad), lambda bb: (bb, 0, 0)),
        out_shape=jax.ShapeDtypeStruct((b, ne, cap_pad), jnp.int32),
        scratch_shapes=[pltpu.VMEM((ne, n), F32)],
        compiler_params=_params(("arbitrary",)),
        name="expert_topk",
    )(probs_t)


def _moe_body(cap, nb, idx_ref, idxn_ref, h2_hbm, wr_ref, wg_ref, wu_ref, wd_ref, ye_ref, xbuf, sem):
    e = pl.program_id(0)
    b = pl.program_id(1)
    step = e * nb + b
    nstep = pl.num_programs(0) * nb
    slot = step % 2

    def start_gather(idx_smem, bb, sl):
        def body(g, carry):
            for r in range(8):
                t = idx_smem[0, 0, g * 8 + r]
                pltpu.make_async_copy(h2_hbm.at[bb, pl.ds(t, 1), :], xbuf.at[sl, g, pl.ds(r, 1), :],
                                      sem.at[sl]).start()
            return carry
        lax.fori_loop(0, cap // 8, body, 0)

    @pl.when(step == 0)
    def _():
        start_gather(idx_ref, b, slot)

    @pl.when(step + 1 < nstep)
    def _():
        nxt = step + 1
        start_gather(idxn_ref, nxt % nb, 1 - slot)

    pltpu.make_async_copy(xbuf.at[slot], xbuf.at[slot], sem.at[slot]).wait()

    xe = xbuf[slot].reshape(cap, xbuf.shape[-1]).astype(BF16)
    logits = jnp.dot(xe, wr_ref[...], preferred_element_type=F32)
    pe = jnp.exp(logits - jnp.max(logits, axis=-1, keepdims=True))
    probs = pe / jnp.sum(pe, axis=-1, keepdims=True)
    col = lax.broadcasted_iota(jnp.int32, probs.shape, 1)
    gate = jnp.sum(jnp.where(col == e, probs, 0.0), axis=-1, keepdims=True)
    ff = wg_ref.shape[2]
    ye = jnp.zeros((cap, xe.shape[1]), F32)
    for f0 in range(0, ff, FF_CHUNK):
        hg = jnp.dot(xe, wg_ref[0, :, f0:f0 + FF_CHUNK], preferred_element_type=F32)
        hu = jnp.dot(xe, wu_ref[0, :, f0:f0 + FF_CHUNK], preferred_element_type=F32)
        hid = (hg * jax.nn.sigmoid(hg) * hu).astype(BF16)
        ye = ye + jnp.dot(hid, wd_ref[0, f0:f0 + FF_CHUNK, :], preferred_element_type=F32)
    ye_ref[0, 0] = ye * gate


def moe_experts(h2, idx, cap, wr, wg, wu, wd):
    b, n, d = h2.shape
    ne, cap_pad = idx.shape[1:]
    ff = wg.shape[-1]
    idx3 = idx.reshape(b * ne, 1, cap_pad)

    def cur(e, bb):
        return (bb * ne + e, 0, 0)

    def nxt(e, bb):
        step = jnp.minimum(e * b + bb + 1, ne * b - 1)
        return ((step % b) * ne + step // b, 0, 0)

    return pl.pallas_call(
        functools.partial(_moe_body, cap, b),
        grid=(ne, b),
        in_specs=[
            pl.BlockSpec((1, 1, cap_pad), cur, memory_space=pltpu.SMEM),
            pl.BlockSpec((1, 1, cap_pad), nxt, memory_space=pltpu.SMEM),
            pl.BlockSpec(memory_space=pl.ANY),
            pl.BlockSpec((d, ne), lambda e, bb: (0, 0)),
            pl.BlockSpec((1, d, ff), lambda e, bb: (e, 0, 0)),
            pl.BlockSpec((1, d, ff), lambda e, bb: (e, 0, 0)),
            pl.BlockSpec((1, ff, d), lambda e, bb: (e, 0, 0)),
        ],
        out_specs=pl.BlockSpec((1, 1, cap, d), lambda e, bb: (bb, e, 0, 0)),
        out_shape=jax.ShapeDtypeStruct((b, ne, cap, d), F32),
        scratch_shapes=[pltpu.VMEM((2, cap // 8, 8, d), F32), pltpu.SemaphoreType.DMA((2,))],
        compiler_params=_params(("arbitrary", "arbitrary")),
        name="moe_experts",
    )(idx3, idx3, h2, wr, wg, wu, wd)


def _combine_body(cap, idx_ref, ye_ref, acc_ref):
    e = pl.program_id(2)

    @pl.when(e == 0)
    def _():
        acc_ref[...] = jnp.zeros(acc_ref.shape, F32)

    def body(s, carry):
        t = idx_ref[0, 0, s]
        acc_ref[0, pl.ds(t, 1), :] = acc_ref[0, pl.ds(t, 1), :] + ye_ref[0, 0, pl.ds(s, 1), :]
        return carry
    lax.fori_loop(0, cap, body, 0, unroll=8 if cap % 8 == 0 else 1)


def moe_combine(ye, idx, n, dh=512):
    b, ne, cap, d = ye.shape
    cap_pad = idx.shape[2]
    idx3 = idx.reshape(b * ne, 1, cap_pad)
    return pl.pallas_call(
        functools.partial(_combine_body, cap),
        grid=(b, d // dh, ne),
        in_specs=[
            pl.BlockSpec((1, 1, cap_pad), lambda bb, j, e: (bb * ne + e, 0, 0), memory_space=pltpu.SMEM),
            pl.BlockSpec((1, 1, cap, dh), lambda bb, j, e: (bb, e, 0, j)),
        ],
        out_specs=pl.BlockSpec((1, n, dh), lambda bb, j, e: (bb, 0, j)),
        out_shape=jax.ShapeDtypeStruct((b, n, d), F32),
        compiler_params=_params(("arbitrary", "arbitrary", "arbitrary")),
        name="moe_combine",
    )(idx3, ye)


def _resid_body(x_ref, y_ref, mod_ref, ng_ref, o_ref):
    y = y_ref[0]
    yn = y * lax.rsqrt(jnp.mean(y * y, axis=-1, keepdims=True) + RMS_EPS) * ng_ref[...]
    o_ref[0] = x_ref[0] + mod_ref[0, 5:6, :] * yn


def moe_residual(x, y, modv, ng3, tm):
    b, n, d = x.shape
    row = lambda bb, j: (bb, j, 0)
    return pl.pallas_call(
        _resid_body,
        grid=(b, n // tm),
        in_specs=[pl.BlockSpec((1, tm, d), row), pl.BlockSpec((1, tm, d), row),
                  pl.BlockSpec((1, 6, d), lambda bb, j: (bb, 0, 0)), pl.BlockSpec((1, d), lambda bb, j: (0, 0))],
        out_specs=pl.BlockSpec((1, tm, d), row),
        out_shape=jax.ShapeDtypeStruct((b, n, d), F32),
        compiler_params=_params(("arbitrary", "arbitrary")),
        name="moe_residual",
    )(x, y, modv, ng3)


def _rope_tables(n):
    rows = n // GRID_W
    row = jnp.repeat(jnp.arange(rows), GRID_W).astype(F32)
    col = jnp.tile(jnp.arange(GRID_W), rows).astype(F32)
    inv = ROPE_THETA ** (-jnp.arange(ROPE_FREQS, dtype=F32) / ROPE_FREQS)
    ar, ac = row[:, None] * inv, col[:, None] * inv
    cos64 = jnp.concatenate([jnp.cos(ar), jnp.cos(ar), jnp.cos(ac), jnp.cos(ac)], axis=1)
    sin64 = jnp.concatenate([-jnp.sin(ar), jnp.sin(ar), -jnp.sin(ac), jnp.sin(ac)], axis=1)
    return jnp.tile(cos64, (1, 2)), jnp.tile(sin64, (1, 2))


def _moe_sublayer(x_mid, h2, probs_t, modv, ng3, wr, wg, wu, wd, tm):
    n = x_mid.shape[1]
    cap = CAP_FACTOR * n // N_EXP
    idx = expert_topk(probs_t, cap)
    ye = moe_experts(h2, idx, cap, wr, wg, wu, wd)
    y = moe_combine(ye, idx, n)
    return moe_residual(x_mid, y, modv, ng3, tm)


def _forward(x, c, ctx, c_ctx, w_mod, b_mod, norm_g, w_in, diff_lambda, diff_subln_g, s5_lam_re, s5_lam_im,
             s5_log_dt, s5_b_re, s5_b_im, s5_c_re, s5_c_im, s5_d, w_s5_glu, gmlp_norm_g, gmlp_ws, gmlp_bs,
             w_br_att, w_br_gmlp, w_out, w_router, w_e_gate, w_e_up, w_e_down, *, tm, tq, tk, t_chunk):
    b, n, d = x.shape
    n_ctx = ctx.shape[1]
    depth = w_mod.shape[0]
    tm_c = min(tm, n_ctx)
    tq_c = min(tq, n_ctx)

    c_rows = jnp.zeros((MOD_ROWS, d), F32).at[:b].set(c).at[b].set(c_ctx)
    mod_all = modulation(c_rows, w_mod, b_mod).reshape(depth, MOD_ROWS, 6, d)
    cos_t, sin_t = _rope_tables(n)
    cos_c, sin_c = jnp.ones((n_ctx, 128), F32), jnp.zeros((n_ctx, 128), F32)

    xc = ctx
    for l in range(depth):
        last = l == depth - 1
        lam_init = 0.8 - 0.6 * math.exp(-0.3 * l)
        modv = mod_all[l, :b]
        modc = jnp.broadcast_to(mod_all[l, b:b + 1], (b, 6, d))
        ng0 = norm_g[l, 0:1]
        w_in_b = w_in[l].astype(BF16)
        gg = gmlp_norm_g[l][None, :]
        ws = gmlp_ws[l].astype(BF16)
        bs = jnp.repeat(gmlp_bs[l].T, GM_HD, axis=1)
        sg = diff_subln_g[l][None, :]
        wr = w_router[l].astype(BF16)
        wg, wu, wd = w_e_gate[l].astype(BF16), w_e_up[l].astype(BF16), w_e_down[l].astype(BF16)

        q, k, v, u, ogm, gates = input_projection(x, modv, ng0, w_in_b, cos_t, sin_t, gg, ws, bs, True, tm)
        pc = input_projection(xc, modc, ng0, w_in_b, cos_c, sin_c, gg, ws, bs, not last, tm_c)
        if last:
            kc, vc, uc = pc
        else:
            qc, kc, vc, uc, ogm_c, gates_c = pc

        o_att = diff_attention(q, kc, vc, k, v, diff_lambda[l], sg, lam_init, tq, tk)

        wb, a, wc = s5_weights(s5_lam_re[l], s5_lam_im[l], s5_log_dt[l], s5_b_re[l], s5_b_im[l],
                               s5_c_re[l], s5_c_im[l])
        u_all = jnp.concatenate([uc, u], axis=1).transpose(1, 0, 2)
        y5 = s5_scan(u_all, wb, a, wc, s5_d[l][None, :], n_ctx, min(t_chunk, n_ctx)).transpose(0, 2, 1, 3)

        merge_w = (norm_g[l, 1:3], w_br_att[l].astype(BF16), w_s5_glu[l].astype(BF16),
                   w_br_gmlp[l].astype(BF16), w_out[l].astype(BF16), w_router[l].T.astype(BF16))
        x_mid, h2, probs_t = merge_branches(x, o_att, y5[0, :, n_ctx:], y5[1, :, n_ctx:], ogm, gates, modv,
                                            *merge_w, tm)
        x = _moe_sublayer(x_mid, h2, probs_t, modv, norm_g[l, 3:4], wr, wg, wu, wd, tm)
        if not last:
            o_att_c = diff_attention(qc, kc, vc, None, None, diff_lambda[l], sg, lam_init, tq_c, tk)
            xc_mid, h2c, probs_c = merge_branches(xc, o_att_c, y5[0, :, :n_ctx], y5[1, :, :n_ctx], ogm_c,
                                                  gates_c, modc, *merge_w, tm_c)
            xc = _moe_sublayer(xc_mid, h2c, probs_c, modc, norm_g[l, 3:4], wr, wg, wu, wd, tm_c)
    return x


def kernel(x, c, ctx, c_ctx, w_mod, b_mod, norm_g, w_in, diff_lambda, diff_subln_g, s5_lam_re, s5_lam_im, s5_log_dt, s5_b_re, s5_b_im, s5_c_re, s5_c_im, s5_d, w_s5_glu, gmlp_norm_g, gmlp_ws, gmlp_bs, w_br_att, w_br_gmlp, w_out, w_router, w_e_gate, w_e_up, w_e_down):
    return _forward(x, c, ctx, c_ctx, w_mod, b_mod, norm_g, w_in, diff_lambda, diff_subln_g, s5_lam_re, s5_lam_im,
                    s5_log_dt, s5_b_re, s5_b_im, s5_c_re, s5_c_im, s5_d, w_s5_glu, gmlp_norm_g, gmlp_ws, gmlp_bs,
                    w_br_att, w_br_gmlp, w_out, w_router, w_e_gate, w_e_up, w_e_down,
                    tm=512, tq=512, tk=512, t_chunk=256)
```

```python
import functools
import math

import jax
import jax.numpy as jnp
from jax import lax
from jax.experimental import pallas as pl
from jax.experimental.pallas import tpu as pltpu

F32 = jnp.float32
BF16 = jnp.bfloat16

D_MODEL = 1024
GRID_W = 64
RMS_EPS = 1e-6
N_HEADS = 4
HEAD_DIM = 64
V_DIM = 2 * HEAD_DIM
ATT_W = N_HEADS * V_DIM
ROPE_THETA = 10000.0
ROPE_FREQS = HEAD_DIM // 4
S5_W = D_MODEL // 4
S5_P = 16
S5_G = S5_W // S5_P
S5_N = 64
S5_LANES = S5_G * S5_N
GM_W = D_MODEL // 4
GM_H = 4
GM_HD = GM_W // GM_H
CHUNK = 128
N_EXP = 16
CAP_FACTOR = 2
K_END = ATT_W
V_END = K_END + ATT_W
S5_END = V_END + S5_W
Q_END = S5_END + ATT_W
GMLP_END = Q_END + 2 * GM_W
IN_COLS = GMLP_END + 3 * D_MODEL
LOG2E = 1.4426950408889634

V7X_VMEM_BYTES = 64 * 1024 * 1024
VMEM_LIMIT = 56 * 1024 * 1024
FF_CHUNK = 512
ATT_PIECE = 256
VT_ROWS = V_DIM + 16
MOD_ROWS = 16


def _params(sem):
    return pltpu.CompilerParams(dimension_semantics=sem, vmem_limit_bytes=VMEM_LIMIT)


def _mod_body(c_ref, w_ref, b_ref, o_ref):
    c = c_ref[...]
    s = (c * jax.nn.sigmoid(c)).astype(BF16)
    o_ref[0] = jnp.dot(s, w_ref[0].astype(BF16), preferred_element_type=F32) + b_ref[0]


def modulation(c_rows, w_mod, b_mod, tn=1536):
    depth, d, n6 = w_mod.shape
    return pl.pallas_call(
        _mod_body,
        grid=(depth, n6 // tn),
        in_specs=[
            pl.BlockSpec((MOD_ROWS, d), lambda l, j: (0, 0)),
            pl.BlockSpec((1, d, tn), lambda l, j: (l, 0, j)),
            pl.BlockSpec((1, 1, tn), lambda l, j: (l, 0, j)),
        ],
        out_specs=pl.BlockSpec((1, MOD_ROWS, tn), lambda l, j: (l, 0, j)),
        out_shape=jax.ShapeDtypeStruct((depth, MOD_ROWS, n6), F32),
        compiler_params=_params(("arbitrary", "arbitrary")),
        name="modulation",
    )(c_rows, w_mod, b_mod.reshape(depth, 1, n6))


def _rope_cols(p, cos, sin):
    lane = lax.broadcasted_iota(jnp.int32, (p.shape[0], 128), 1)
    first = (lane % 32) < ROPE_FREQS
    outs = []
    for s in range(p.shape[1] // 128):
        ps = p[:, s * 128:(s + 1) * 128]
        sw = jnp.where(first, pltpu.roll(ps, 128 - ROPE_FREQS, 1), pltpu.roll(ps, ROPE_FREQS, 1))
        outs.append(ps * cos + sw * sin)
    return jnp.concatenate(outs, axis=1)


def _inproj_body(full, x_ref, mod_ref, ng_ref, w_ref, cos_ref, sin_ref, gg_ref, ws_ref, bs_ref, *outs):
    xt = x_ref[0]
    h = xt * lax.rsqrt(jnp.mean(xt * xt, axis=-1, keepdims=True) + RMS_EPS) * ng_ref[...]
    h = h * (1.0 + mod_ref[0, 1:2, :]) + mod_ref[0, 0:1, :]
    hb = h.astype(BF16)

    def proj(a, b):
        return jnp.dot(hb, w_ref[:, a:b], preferred_element_type=F32)

    cos = cos_ref[...]
    sin = sin_ref[...]
    if full:
        q_ref, k_ref, v_ref, u_ref, ogm_ref, gate_ref = outs
    else:
        k_ref, v_ref, u_ref = outs
    k_ref[0] = _rope_cols(proj(0, K_END), cos, sin).astype(BF16)
    vt = proj(K_END, V_END).T
    pad_row = lax.broadcasted_iota(jnp.int32, (VT_ROWS - V_DIM, vt.shape[1]), 0)
    ones_pad = jnp.where(pad_row == 0, 1.0, 0.0).astype(BF16)
    for hh in range(N_HEADS):
        v_ref[0, hh * VT_ROWS:hh * VT_ROWS + V_DIM, :] = vt[hh * V_DIM:(hh + 1) * V_DIM, :].astype(BF16)
        v_ref[0, hh * VT_ROWS + V_DIM:(hh + 1) * VT_ROWS, :] = ones_pad
    u_ref[0] = proj(V_END, S5_END)
    if not full:
        return
    q_ref[0] = (_rope_cols(proj(S5_END, Q_END), cos, sin) * (HEAD_DIM ** -0.5 * LOG2E)).astype(BF16)

    zg = jax.nn.gelu(proj(Q_END, GMLP_END))
    zu = zg[:, :GM_W]
    zv = zg[:, GM_W:]
    mu = jnp.mean(zv, axis=-1, keepdims=True)
    zc = zv - mu
    zvn = (zc * lax.rsqrt(jnp.mean(zc * zc, axis=-1, keepdims=True) + RMS_EPS) * gg_ref[...]).astype(BF16)
    head = lax.broadcasted_iota(jnp.int32, (CHUNK, GM_W), 1) // GM_HD
    tm = xt.shape[0]
    for c in range(tm // CHUNK):
        zc_blk = zvn[c * CHUNK:(c + 1) * CHUNK, :]
        mixed = bs_ref[...]
        for g in range(GM_H):
            mixed = mixed + jnp.dot(ws_ref[g], jnp.where(head == g, zc_blk, jnp.zeros_like(zc_blk)),
                                    preferred_element_type=F32)
        ogm_ref[0, c * CHUNK:(c + 1) * CHUNK, :] = (zu[c * CHUNK:(c + 1) * CHUNK, :] * mixed).astype(BF16)

    for j in range(3):
        a = GMLP_END + j * D_MODEL
        gate_ref[0, :, j * D_MODEL:(j + 1) * D_MODEL] = jax.nn.sigmoid(proj(a, a + D_MODEL)).astype(BF16)


def input_projection(x, modv, ng, w_in, cos_t, sin_t, gg, ws, bs, full, tm):
    b, n, d = x.shape
    assert n % tm == 0 and tm % CHUNK == 0
    row = lambda bb, j: (bb, j, 0)
    const2 = lambda bb, j: (0, 0)
    out_shape = [jax.ShapeDtypeStruct((b, n, ATT_W), BF16),
                 jax.ShapeDtypeStruct((b, N_HEADS * VT_ROWS, n), BF16),
                 jax.ShapeDtypeStruct((b, n, S5_W), F32)]
    out_specs = [pl.BlockSpec((1, tm, ATT_W), row),
                 pl.BlockSpec((1, N_HEADS * VT_ROWS, tm), lambda bb, j: (bb, 0, j)),
                 pl.BlockSpec((1, tm, S5_W), row)]
    if full:
        out_shape = [jax.ShapeDtypeStruct((b, n, ATT_W), BF16)] + out_shape + [
            jax.ShapeDtypeStruct((b, n, GM_W), BF16), jax.ShapeDtypeStruct((b, n, 3 * d), BF16)]
        out_specs = [pl.BlockSpec((1, tm, ATT_W), row)] + out_specs + [
            pl.BlockSpec((1, tm, GM_W), row), pl.BlockSpec((1, tm, 3 * d), row)]
    return pl.pallas_call(
        functools.partial(_inproj_body, full),
        grid=(b, n // tm),
        in_specs=[
            pl.BlockSpec((1, tm, d), row),
            pl.BlockSpec((1, 6, d), lambda bb, j: (bb, 0, 0)),
            pl.BlockSpec((1, d), const2),
            pl.BlockSpec((d, IN_COLS), const2),
            pl.BlockSpec((tm, 128), lambda bb, j: (j, 0)),
            pl.BlockSpec((tm, 128), lambda bb, j: (j, 0)),
            pl.BlockSpec((1, GM_W), const2),
            pl.BlockSpec((GM_H, CHUNK, CHUNK), lambda bb, j: (0, 0, 0)),
            pl.BlockSpec((CHUNK, GM_W), const2),
        ],
        out_specs=out_specs,
        out_shape=out_shape,
        compiler_params=_params(("arbitrary", "arbitrary")),
        name="input_projection" if full else "input_projection_ctx_side",
    )(x, modv, ng, w_in, cos_t, sin_t, gg, ws, bs)


def _attn_body(lam_init, tk, has_latent, q_ref, kc_ref, vtc_ref, k_ref, vt_ref, dl_ref, sg_ref, o_ref,
               m_ref, acc_ref, q2_ref, s_ref, p_ref, al_ref, bm_ref):
    tq = q_ref.shape[1]
    nc = kc_ref.shape[1]
    n_strip = 2 * tq // 128
    q = q_ref[0]
    lane = lax.broadcasted_iota(jnp.int32, q.shape, 1)
    zero = jnp.zeros_like(q)
    q2_ref[0:tq, :] = jnp.where(lane < HEAD_DIM, q, zero)
    q2_ref[tq:2 * tq, :] = jnp.where(lane >= HEAD_DIM, q, zero)
    m_ref[...] = jnp.full(m_ref.shape, -jnp.inf, F32)
    acc_ref[...] = jnp.zeros(acc_ref.shape, F32)

    def qk(kb, slot, rows):
        st = lax.dot_general(kb, q2_ref[...], (((1,), (1,)), ((), ())), preferred_element_type=F32)
        for c in range(n_strip):
            s_ref[slot, c, 0:rows, :] = st[:, c * 128:(c + 1) * 128]
        bm_ref[slot] = jnp.max(st, axis=0, keepdims=True)

    def sm(slot, rows):
        m_old = m_ref[...]
        m_new = jnp.maximum(m_old, bm_ref[slot])
        al_ref[slot] = jnp.exp2(m_old - m_new)
        m_ref[...] = m_new
        piece = min(ATT_PIECE, rows)
        for c in range(n_strip):
            for r0 in range(0, rows, piece):
                p_ref[slot, c, r0:r0 + piece, :] = jnp.exp2(
                    s_ref[slot, c, r0:r0 + piece, :] - m_new[:, c * 128:(c + 1) * 128]).astype(BF16)

    def pv(vtb, slot, rows):
        p = jnp.concatenate([p_ref[slot, c, 0:rows, :] for c in range(n_strip)], axis=1)
        acc_ref[...] = al_ref[slot] * acc_ref[...] + jnp.dot(vtb, p, preferred_element_type=F32)

    def k_blk(j):
        return k_ref[0, pl.ds(pl.multiple_of(j * tk, tk), tk), :]

    def vt_blk(j):
        return vt_ref[0, :, pl.ds(pl.multiple_of(j * tk, tk), tk)]

    qk(kc_ref[0], 0, nc)
    sm(0, nc)
    if not has_latent:
        pv(vtc_ref[0], 0, nc)
    else:
        n_kv = k_ref.shape[1] // tk
        assert n_kv >= 4 and n_kv % 2 == 0
        qk(k_blk(0), 1, tk)
        pv(vtc_ref[0], 0, nc)
        qk(k_blk(1), 0, tk)
        sm(1, tk)

        def body(i, carry):
            pv(vt_blk(2 * i), 1, tk)
            qk(k_blk(2 * i + 2), 1, tk)
            sm(0, tk)
            pv(vt_blk(2 * i + 1), 0, tk)
            qk(k_blk(2 * i + 3), 0, tk)
            sm(1, tk)
            return carry
        lax.fori_loop(0, (n_kv - 2) // 2, body, 0)
        pv(vt_blk(n_kv - 2), 1, tk)
        sm(0, tk)
        pv(vt_blk(n_kv - 1), 0, tk)

    dl = dl_ref[...]
    lam = (jnp.exp(jnp.sum(dl[0:1, :] * dl[1:2, :], axis=-1, keepdims=True))
           - jnp.exp(jnp.sum(dl[2:3, :] * dl[3:4, :], axis=-1, keepdims=True)) + lam_init)
    num = acc_ref[0:V_DIM, :]
    den = acc_ref[V_DIM:V_DIM + 1, :]
    ot = num[:, 0:tq] / den[:, 0:tq] - lam * (num[:, tq:2 * tq] / den[:, tq:2 * tq])
    ot = ot * lax.rsqrt(jnp.mean(ot * ot, axis=0, keepdims=True) + RMS_EPS)
    o_ref[0] = (ot.T * sg_ref[...] * (1.0 - lam_init)).astype(BF16)


def diff_attention(q, kc, vtc, k, vt, dlam, sg, lam_init, tq, tk):
    b, nq, _ = q.shape
    nc = kc.shape[1]
    has_latent = k is not None
    if not has_latent:
        k, vt = kc, vtc
    n = k.shape[1]
    assert nq % tq == 0 and (not has_latent or n % tk == 0)
    head_rows = lambda bb, h, i: (bb, i, h)
    head_all = lambda bb, h, i: (bb, 0, h)
    head_all_t = lambda bb, h, i: (bb, h, 0)
    return pl.pallas_call(
        functools.partial(_attn_body, lam_init, tk, has_latent),
        grid=(b, N_HEADS, nq // tq),
        in_specs=[
            pl.BlockSpec((1, tq, V_DIM), head_rows),
            pl.BlockSpec((1, nc, V_DIM), head_all),
            pl.BlockSpec((1, VT_ROWS, nc), head_all_t),
            pl.BlockSpec((1, n, V_DIM), head_all),
            pl.BlockSpec((1, VT_ROWS, n), head_all_t),
            pl.BlockSpec((4, HEAD_DIM), lambda bb, h, i: (0, 0)),
            pl.BlockSpec((1, V_DIM), lambda bb, h, i: (0, 0)),
        ],
        out_specs=pl.BlockSpec((1, tq, V_DIM), head_rows),
        out_shape=jax.ShapeDtypeStruct((b, nq, ATT_W), BF16),
        scratch_shapes=[pltpu.VMEM((1, 2 * tq), F32), pltpu.VMEM((VT_ROWS, 2 * tq), F32),
                        pltpu.VMEM((2 * tq, V_DIM), BF16),
                        pltpu.VMEM((2, 2 * tq // 128, max(tk, nc), 128), F32),
                        pltpu.VMEM((2, 2 * tq // 128, max(tk, nc), 128), BF16),
                        pltpu.VMEM((2, 1, 2 * tq), F32), pltpu.VMEM((2, 1, 2 * tq), F32)],
        compiler_params=_params(("arbitrary", "arbitrary", "arbitrary")),
        name="diff_attention" if has_latent else "diff_attention_ctx",
    )(q, kc, vtc, k, vt, dlam, sg)


def _s5_body(t_chunk, u_ref, wb_ref, a_ref, wc_ref, dsk_ref, y_ref, xs_ref, st_ref):
    d = pl.program_id(0)
    kk = pl.program_id(1)
    nb = u_ref.shape[1]

    @pl.when(kk == 0)
    def _():
        st_ref[...] = jnp.zeros(st_ref.shape, F32)

    u = u_ref[...].reshape(t_chunk * nb, S5_W)
    xs_ref[...] = jnp.dot(u.astype(BF16), wb_ref[0], preferred_element_type=F32).reshape(
        t_chunk, nb, 2 * S5_LANES)
    ar = jnp.broadcast_to(a_ref[0, 0:1, :], (nb, S5_LANES))
    ai = jnp.broadcast_to(a_ref[0, 1:2, :], (nb, S5_LANES))

    def body(i, carry):
        sr, si = carry
        t = jnp.where(d == 0, i, t_chunk - 1 - i)
        xr = xs_ref[t, :, 0:S5_LANES]
        xi = xs_ref[t, :, S5_LANES:2 * S5_LANES]
        nr = ar * sr - ai * si + xr
        ni = ar * si + ai * sr + xi
        xs_ref[t, :, 0:S5_LANES] = nr
        xs_ref[t, :, S5_LANES:2 * S5_LANES] = ni
        return nr, ni

    sr, si = lax.fori_loop(0, t_chunk, body, (st_ref[0], st_ref[1]), unroll=4)
    st_ref[0] = sr
    st_ref[1] = si
    y = jnp.dot(xs_ref[...].reshape(t_chunk * nb, 2 * S5_LANES).astype(BF16), wc_ref[0],
                preferred_element_type=F32)
    y = y + jnp.where(d == 0, 1.0, 0.0) * (dsk_ref[...] * u)
    y_ref[0] = y.reshape(t_chunk, nb, S5_W)


def s5_scan(u_all, wb, a, wc, dskip, n_ctx, t_chunk):
    tt, nb, _ = u_all.shape
    assert n_ctx % t_chunk == 0 and tt % t_chunk == 0
    nchunk = tt // t_chunk
    cctx = n_ctx // t_chunk

    def chunk_index(d, kk):
        bwd = jnp.where(kk < cctx, cctx - 1 - kk, nchunk - 1 - (kk - cctx))
        return jnp.where(d == 0, kk, bwd)

    return pl.pallas_call(
        functools.partial(_s5_body, t_chunk),
        grid=(2, nchunk),
        in_specs=[
            pl.BlockSpec((t_chunk, nb, S5_W), lambda d, kk: (chunk_index(d, kk), 0, 0)),
            pl.BlockSpec((1, S5_W, 2 * S5_LANES), lambda d, kk: (d, 0, 0)),
            pl.BlockSpec((1, 2, S5_LANES), lambda d, kk: (d, 0, 0)),
            pl.BlockSpec((1, 2 * S5_LANES, S5_W), lambda d, kk: (d, 0, 0)),
            pl.BlockSpec((1, S5_W), lambda d, kk: (0, 0)),
        ],
        out_specs=pl.BlockSpec((1, t_chunk, nb, S5_W), lambda d, kk: (d, chunk_index(d, kk), 0, 0)),
        out_shape=jax.ShapeDtypeStruct((2, tt, nb, S5_W), F32),
        scratch_shapes=[pltpu.VMEM((t_chunk, nb, 2 * S5_LANES), F32), pltpu.VMEM((2, nb, S5_LANES), F32)],
        compiler_params=_params(("arbitrary", "arbitrary")),
        name="s5_scan",
    )(u_all, wb, a, wc, dskip)


def s5_weights(lam_re, lam_im, log_dt, b_re, b_im, c_re, c_im):
    dt = jnp.exp(log_dt.astype(F32))[..., None]
    lr, li = lam_re.astype(F32), lam_im.astype(F32)
    mag = jnp.exp(lr * dt)
    ar, ai = mag * jnp.cos(li * dt), mag * jnp.sin(li * dt)
    den = lr * lr + li * li
    fr = ((ar - 1.0) * lr + ai * li) / den
    fi = (ai * lr - (ar - 1.0) * li) / den
    bbr = fr[..., None] * b_re - fi[..., None] * b_im
    bbi = fr[..., None] * b_im + fi[..., None] * b_re
    eye = jnp.eye(S5_G, dtype=F32)
    def in_map(bb):
        return jnp.einsum('dgnc,gh->dgchn', bb, eye).reshape(2, S5_W, S5_LANES)
    wb = jnp.concatenate([in_map(bbr), in_map(bbi)], axis=-1).astype(BF16)
    def out_map(cc):
        return jnp.einsum('dgcn,gh->dgnhc', cc, eye).reshape(2, S5_LANES, S5_W)
    wc = jnp.concatenate([out_map(c_re.astype(F32)), out_map(-c_im.astype(F32))], axis=1).astype(BF16)
    a = jnp.stack([ar.reshape(2, S5_LANES), ai.reshape(2, S5_LANES)], axis=1)
    return wb, a, wc


def _merge_body(x_ref, oatt_ref, yf_ref, yb_ref, ogm_ref, gate_ref, mod_ref, ng_ref, watt_ref, wglu_ref,
                wgm_ref, wout_ref, wr_ref, xo_ref, h2_ref, pr_ref):
    y_att = jnp.dot(oatt_ref[0], watt_ref[...], preferred_element_type=F32)
    y5 = yf_ref[0] + yb_ref[0]
    z = jnp.dot(jax.nn.gelu(y5).astype(BF16), wglu_ref[...], preferred_element_type=F32)
    y5g = z[:, :D_MODEL] * jax.nn.sigmoid(z[:, D_MODEL:])
    y_gm = jnp.dot(ogm_ref[0], wgm_ref[...], preferred_element_type=F32)
    merged = (gate_ref[0, :, 0:D_MODEL].astype(F32) * y_att
              + gate_ref[0, :, D_MODEL:2 * D_MODEL].astype(F32) * y5g
              + gate_ref[0, :, 2 * D_MODEL:3 * D_MODEL].astype(F32) * y_gm)
    mix = jnp.dot(merged.astype(BF16), wout_ref[...], preferred_element_type=F32)
    mixn = mix * lax.rsqrt(jnp.mean(mix * mix, axis=-1, keepdims=True) + RMS_EPS) * ng_ref[0:1, :]
    xn = x_ref[0] + mod_ref[0, 2:3, :] * mixn
    xo_ref[0] = xn
    h2 = xn * lax.rsqrt(jnp.mean(xn * xn, axis=-1, keepdims=True) + RMS_EPS) * ng_ref[1:2, :]
    h2 = h2 * (1.0 + mod_ref[0, 4:5, :]) + mod_ref[0, 3:4, :]
    h2_ref[0] = h2
    logits = lax.dot_general(wr_ref[...], h2.astype(BF16), (((1,), (1,)), ((), ())),
                             preferred_element_type=F32)
    pe = jnp.exp(logits - jnp.max(logits, axis=0, keepdims=True))
    pr_ref[0] = pe / jnp.sum(pe, axis=0, keepdims=True)


def merge_branches(x, o_att, yf, yb, ogm, gates, modv, ng12, watt, wglu, wgm, wout, wr_t, tm):
    b, n, d = x.shape
    row = lambda bb, j: (bb, j, 0)
    const2 = lambda bb, j: (0, 0)
    return pl.pallas_call(
        _merge_body,
        grid=(b, n // tm),
        in_specs=[
            pl.BlockSpec((1, tm, d), row),
            pl.BlockSpec((1, tm, ATT_W), row),
            pl.BlockSpec((1, tm, S5_W), row),
            pl.BlockSpec((1, tm, S5_W), row),
            pl.BlockSpec((1, tm, GM_W), row),
            pl.BlockSpec((1, tm, 3 * d), row),
            pl.BlockSpec((1, 6, d), lambda bb, j: (bb, 0, 0)),
            pl.BlockSpec((2, d), const2),
            pl.BlockSpec((ATT_W, d), const2),
            pl.BlockSpec((S5_W, 2 * d), const2),
            pl.BlockSpec((GM_W, d), const2),
            pl.BlockSpec((d, d), const2),
            pl.BlockSpec((N_EXP, d), const2),
        ],
        out_specs=[pl.BlockSpec((1, tm, d), row), pl.BlockSpec((1, tm, d), row),
                   pl.BlockSpec((1, N_EXP, tm), lambda bb, j: (bb, 0, j))],
        out_shape=[jax.ShapeDtypeStruct((b, n, d), F32), jax.ShapeDtypeStruct((b, n, d), F32),
                   jax.ShapeDtypeStruct((b, N_EXP, n), F32)],
        compiler_params=_params(("arbitrary", "arbitrary")),
        name="merge_branches",
    )(x, o_att, yf, yb, ogm, gates, modv, ng12, watt, wglu, wgm, wout, wr_t)


def _cumsum_lanes(x, tri):
    outs = []
    run = jnp.zeros((x.shape[0], 1), F32)
    for c in range(x.shape[1] // 128):
        blk = jnp.dot(x[:, c * 128:(c + 1) * 128].astype(BF16), tri, preferred_element_type=F32) + run
        outs.append(blk)
        run = blk[:, 127:128]
    return jnp.concatenate(outs, axis=1)


def _topk_body(cap, p_ref, idx_ref, cs_ref):
    p = p_ref[0]
    ne, n = p.shape
    bits = pltpu.bitcast(p, jnp.int32)
    thr = jnp.zeros((ne, 1), jnp.int32)
    for bit in range(30, -1, -1):
        cand = thr | (1 << bit)
        cnt = jnp.sum(jnp.where(bits >= cand, 1.0, 0.0), axis=-1, keepdims=True)
        thr = jnp.where(cnt >= cap, cand, thr)
    gt = bits > thr
    eq = bits == thr
    n_gt = jnp.sum(jnp.where(gt, 1.0, 0.0), axis=-1, keepdims=True)
    r = lax.broadcasted_iota(jnp.int32, (128, 128), 0)
    c = lax.broadcasted_iota(jnp.int32, (128, 128), 1)
    tri = jnp.where(r <= c, 1.0, 0.0).astype(BF16)
    eq_f = jnp.where(eq, 1.0, 0.0)
    eq_rank = _cumsum_lanes(eq_f, tri) - eq_f
    sel = jnp.where(gt | (eq & (eq_rank < cap - n_gt)), 1.0, 0.0)
    cs_ref[...] = _cumsum_lanes(sel, tri)
    nblk = idx_ref.shape[2] // 128

    def per_expert(e, carry):
        row = cs_ref[pl.ds(e, 1), :]
        blks = []
        for sb in range(nblk):
            slot = (lax.broadcasted_iota(jnp.int32, (128, 1), 0) + sb * 128).astype(F32)
            cnt = jnp.sum(jnp.where(row <= slot, 1.0, 0.0), axis=-1, keepdims=True)
            blks.append(jnp.broadcast_to(cnt, (128, 128)).T[0:1, :])
        idx_ref[0, pl.ds(e, 1), :] = jnp.concatenate(blks, axis=1).astype(jnp.int32)
        return carry
    lax.fori_loop(0, ne, per_expert, 0)


def expert_topk(probs_t, cap):
    b, ne, n = probs_t.shape
    cap_pad = -(-cap // 128) * 128
    return pl.pallas_call(
        functools.partial(_topk_body, cap),
        grid=(b,),
        in_specs=[pl.BlockSpec((1, ne, n), lambda bb: (bb, 0, 0))],
        out_specs=pl.BlockSpec((1, ne, cap_pad), lambda bb: (bb, 0, 0)),
        out_shape=jax.ShapeDtypeStruct((b, ne, cap_pad), jnp.int32),
        scratch_shapes=[pltpu.VMEM((ne, n), F32)],
        compiler_params=_params(("arbitrary",)),
        name="expert_topk",
    )(probs_t)


def _moe_body(cap, nb, idx_ref, idxn_ref, h2_hbm, wr_ref, wg_ref, wu_ref, wd_ref, ye_ref, xbuf, sem):
    e = pl.program_id(0)
    b = pl.program_id(1)
    step = e * nb + b
    nstep = pl.num_programs(0) * nb
    slot = step % 2

    def start_gather(idx_smem, bb, sl):
        def body(g, carry):
            for r in range(8):
                t = idx_smem[0, 0, g * 8 + r]
                pltpu.make_async_copy(h2_hbm.at[bb, pl.ds(t, 1), :], xbuf.at[sl, g, pl.ds(r, 1), :],
                                      sem.at[sl]).start()
            return carry
        lax.fori_loop(0, cap // 8, body, 0)

    @pl.when(step == 0)
    def _():
        start_gather(idx_ref, b, slot)

    @pl.when(step + 1 < nstep)
    def _():
        nxt = step + 1
        start_gather(idxn_ref, nxt % nb, 1 - slot)

    pltpu.make_async_copy(xbuf.at[slot], xbuf.at[slot], sem.at[slot]).wait()

    xe = xbuf[slot].reshape(cap, xbuf.shape[-1]).astype(BF16)
    logits = jnp.dot(xe, wr_ref[...], preferred_element_type=F32)
    pe = jnp.exp(logits - jnp.max(logits, axis=-1, keepdims=True))
    probs = pe / jnp.sum(pe, axis=-1, keepdims=True)
    col = lax.broadcasted_iota(jnp.int32, probs.shape, 1)
    gate = jnp.sum(jnp.where(col == e, probs, 0.0), axis=-1, keepdims=True)
    ff = wg_ref.shape[2]
    ye = jnp.zeros((cap, xe.shape[1]), F32)
    for f0 in range(0, ff, FF_CHUNK):
        hg = jnp.dot(xe, wg_ref[0, :, f0:f0 + FF_CHUNK], preferred_element_type=F32)
        hu = jnp.dot(xe, wu_ref[0, :, f0:f0 + FF_CHUNK], preferred_element_type=F32)
        hid = (hg * jax.nn.sigmoid(hg) * hu).astype(BF16)
        ye = ye + jnp.dot(hid, wd_ref[0, f0:f0 + FF_CHUNK, :], preferred_element_type=F32)
    ye_ref[0, 0] = ye * gate


def moe_experts(h2, idx, cap, wr, wg, wu, wd):
    b, n, d = h2.shape
    ne, cap_pad = idx.shape[1:]
    ff = wg.shape[-1]
    idx3 = idx.reshape(b * ne, 1, cap_pad)

    def cur(e, bb):
        return (bb * ne + e, 0, 0)

    def nxt(e, bb):
        step = jnp.minimum(e * b + bb + 1, ne * b - 1)
        return ((step % b) * ne + step // b, 0, 0)

    return pl.pallas_call(
        functools.partial(_moe_body, cap, b),
        grid=(ne, b),
        in_specs=[
            pl.BlockSpec((1, 1, cap_pad), cur, memory_space=pltpu.SMEM),
            pl.BlockSpec((1, 1, cap_pad), nxt, memory_space=pltpu.SMEM),
            pl.BlockSpec(memory_space=pl.ANY),
            pl.BlockSpec((d, ne), lambda e, bb: (0, 0)),
            pl.BlockSpec((1, d, ff), lambda e, bb: (e, 0, 0)),
            pl.BlockSpec((1, d, ff), lambda e, bb: (e, 0, 0)),
            pl.BlockSpec((1, ff, d), lambda e, bb: (e, 0, 0)),
        ],
        out_specs=pl.BlockSpec((1, 1, cap, d), lambda e, bb: (bb, e, 0, 0)),
        out_shape=jax.ShapeDtypeStruct((b, ne, cap, d), F32),
        scratch_shapes=[pltpu.VMEM((2, cap // 8, 8, d), F32), pltpu.SemaphoreType.DMA((2,))],
        compiler_params=_params(("arbitrary", "arbitrary")),
        name="moe_experts",
    )(idx3, idx3, h2, wr, wg, wu, wd)


def _combine_body(cap, idx_ref, ye_ref, acc_ref):
    e = pl.program_id(2)

    @pl.when(e == 0)
    def _():
        acc_ref[...] = jnp.zeros(acc_ref.shape, F32)

    def body(s, carry):
        t = idx_ref[0, 0, s]
        acc_ref[0, pl.ds(t, 1), :] = acc_ref[0, pl.ds(t, 1), :] + ye_ref[0, 0, pl.ds(s, 1), :]
        return carry
    lax.fori_loop(0, cap, body, 0, unroll=8 if cap % 8 == 0 else 1)


def moe_combine(ye, idx, n, dh=512):
    b, ne, cap, d = ye.shape
    cap_pad = idx.shape[2]
    idx3 = idx.reshape(b * ne, 1, cap_pad)
    return pl.pallas_call(
        functools.partial(_combine_body, cap),
        grid=(b, d // dh, ne),
        in_specs=[
            pl.BlockSpec((1, 1, cap_pad), lambda bb, j, e: (bb * ne + e, 0, 0), memory_space=pltpu.SMEM),
            pl.BlockSpec((1, 1, cap, dh), lambda bb, j, e: (bb, e, 0, j)),
        ],
        out_specs=pl.BlockSpec((1, n, dh), lambda bb, j, e: (bb, 0, j)),
        out_shape=jax.ShapeDtypeStruct((b, n, d), F32),
        compiler_params=_params(("arbitrary", "arbitrary", "arbitrary")),
        name="moe_combine",
    )(idx3, ye)


def _resid_body(x_ref, y_ref, mod_ref, ng_ref, o_ref):
    y = y_ref[0]
    yn = y * lax.rsqrt(jnp.mean(y * y, axis=-1, keepdims=True) + RMS_EPS) * ng_ref[...]
    o_ref[0] = x_ref[0] + mod_ref[0, 5:6, :] * yn


def moe_residual(x, y, modv, ng3, tm):
    b, n, d = x.shape
    row = lambda bb, j: (bb, j, 0)
    return pl.pallas_call(
        _resid_body,
        grid=(b, n // tm),
        in_specs=[pl.BlockSpec((1, tm, d), row), pl.BlockSpec((1, tm, d), row),
                  pl.BlockSpec((1, 6, d), lambda bb, j: (bb, 0, 0)), pl.BlockSpec((1, d), lambda bb, j: (0, 0))],
        out_specs=pl.BlockSpec((1, tm, d), row),
        out_shape=jax.ShapeDtypeStruct((b, n, d), F32),
        compiler_params=_params(("arbitrary", "arbitrary")),
        name="moe_residual",
    )(x, y, modv, ng3)


def _rope_tables(n):
    rows = n // GRID_W
    row = jnp.repeat(jnp.arange(rows), GRID_W).astype(F32)
    col = jnp.tile(jnp.arange(GRID_W), rows).astype(F32)
    inv = ROPE_THETA ** (-jnp.arange(ROPE_FREQS, dtype=F32) / ROPE_FREQS)
    ar, ac = row[:, None] * inv, col[:, None] * inv
    cos64 = jnp.concatenate([jnp.cos(ar), jnp.cos(ar), jnp.cos(ac), jnp.cos(ac)], axis=1)
    sin64 = jnp.concatenate([-jnp.sin(ar), jnp.sin(ar), -jnp.sin(ac), jnp.sin(ac)], axis=1)
    return jnp.tile(cos64, (1, 2)), jnp.tile(sin64, (1, 2))


def _moe_sublayer(x_mid, h2, probs_t, modv, ng3, wr, wg, wu, wd, tm):
    n = x_mid.shape[1]
    cap = CAP_FACTOR * n // N_EXP
    idx = expert_topk(probs_t, cap)
    ye = moe_experts(h2, idx, cap, wr, wg, wu, wd)
    y = moe_combine(ye, idx, n)
    return moe_residual(x_mid, y, modv, ng3, tm)


def _forward(x, c, ctx, c_ctx, w_mod, b_mod, norm_g, w_in, diff_lambda, diff_subln_g, s5_lam_re, s5_lam_im,
             s5_log_dt, s5_b_re, s5_b_im, s5_c_re, s5_c_im, s5_d, w_s5_glu, gmlp_norm_g, gmlp_ws, gmlp_bs,
             w_br_att, w_br_gmlp, w_out, w_router, w_e_gate, w_e_up, w_e_down, *, tm, tq, tk, t_chunk):
    b, n, d = x.shape
    n_ctx = ctx.shape[1]
    depth = w_mod.shape[0]
    tm_c = min(tm, n_ctx)
    tq_c = min(tq, n_ctx)

    c_rows = jnp.zeros((MOD_ROWS, d), F32).at[:b].set(c).at[b].set(c_ctx)
    mod_all = modulation(c_rows, w_mod, b_mod).reshape(depth, MOD_ROWS, 6, d)
    cos_t, sin_t = _rope_tables(n)
    cos_c, sin_c = jnp.ones((n_ctx, 128), F32), jnp.zeros((n_ctx, 128), F32)

    xc = ctx
    for l in range(depth):
        last = l == depth - 1
        lam_init = 0.8 - 0.6 * math.exp(-0.3 * l)
        modv = mod_all[l, :b]
        modc = jnp.broadcast_to(mod_all[l, b:b + 1], (b, 6, d))
        ng0 = norm_g[l, 0:1]
        w_in_b = w_in[l].astype(BF16)
        gg = gmlp_norm_g[l][None, :]
        ws = gmlp_ws[l].astype(BF16)
        bs = jnp.repeat(gmlp_bs[l].T, GM_HD, axis=1)
        sg = diff_subln_g[l][None, :]
        wr = w_router[l].astype(BF16)
        wg, wu, wd = w_e_gate[l].astype(BF16), w_e_up[l].astype(BF16), w_e_down[l].astype(BF16)

        q, k, v, u, ogm, gates = input_projection(x, modv, ng0, w_in_b, cos_t, sin_t, gg, ws, bs, True, tm)
        pc = input_projection(xc, modc, ng0, w_in_b, cos_c, sin_c, gg, ws, bs, not last, tm_c)
        if last:
            kc, vc, uc = pc
        else:
            qc, kc, vc, uc, ogm_c, gates_c = pc

        o_att = diff_attention(q, kc, vc, k, v, diff_lambda[l], sg, lam_init, tq, tk)

        wb, a, wc = s5_weights(s5_lam_re[l], s5_lam_im[l], s5_log_dt[l], s5_b_re[l], s5_b_im[l],
                               s5_c_re[l], s5_c_im[l])
        u_all = jnp.concatenate([uc, u], axis=1).transpose(1, 0, 2)
        y5 = s5_scan(u_all, wb, a, wc, s5_d[l][None, :], n_ctx, min(t_chunk, n_ctx)).transpose(0, 2, 1, 3)

        merge_w = (norm_g[l, 1:3], w_br_att[l].astype(BF16), w_s5_glu[l].astype(BF16),
                   w_br_gmlp[l].astype(BF16), w_out[l].astype(BF16), w_router[l].T.astype(BF16))
        x_mid, h2, probs_t = merge_branches(x, o_att, y5[0, :, n_ctx:], y5[1, :, n_ctx:], ogm, gates, modv,
                                            *merge_w, tm)
        x = _moe_sublayer(x_mid, h2, probs_t, modv, norm_g[l, 3:4], wr, wg, wu, wd, tm)
        if not last:
            o_att_c = diff_attention(qc, kc, vc, None, None, diff_lambda[l], sg, lam_init, tq_c, tk)
            xc_mid, h2c, probs_c = merge_branches(xc, o_att_c, y5[0, :, :n_ctx], y5[1, :, :n_ctx], ogm_c,
                                                  gates_c, modc, *merge_w, tm_c)
            xc = _moe_sublayer(xc_mid, h2c, probs_c, modc, norm_g[l, 3:4], wr, wg, wu, wd, tm_c)
    return x


def kernel(x, c, ctx, c_ctx, w_mod, b_mod, norm_g, w_in, diff_lambda, diff_subln_g, s5_lam_re, s5_lam_im, s5_log_dt, s5_b_re, s5_b_im, s5_c_re, s5_c_im, s5_d, w_s5_glu, gmlp_norm_g, gmlp_ws, gmlp_bs, w_br_att, w_br_gmlp, w_out, w_router, w_e_gate, w_e_up, w_e_down):
    return _forward(x, c, ctx, c_ctx, w_mod, b_mod, norm_g, w_in, diff_lambda, diff_subln_g, s5_lam_re, s5_lam_im,
                    s5_log_dt, s5_b_re, s5_b_im, s5_c_re, s5_c_im, s5_d, w_s5_glu, gmlp_norm_g, gmlp_ws, gmlp_bs,
                    w_br_att, w_br_gmlp, w_out, w_router, w_e_gate, w_e_up, w_e_down,
                    tm=512, tq=512, tk=512, t_chunk=256)
```

```python
import functools
import math

import jax
import jax.numpy as jnp
from jax import lax
from jax.experimental import pallas as pl
from jax.experimental.pallas import tpu as pltpu

F32 = jnp.float32
BF16 = jnp.bfloat16

D_MODEL = 1024
GRID_W = 64
RMS_EPS = 1e-6
N_HEADS = 4
HEAD_DIM = 64
V_DIM = 2 * HEAD_DIM
ATT_W = N_HEADS * V_DIM
ROPE_THETA = 10000.0
ROPE_FREQS = HEAD_DIM // 4
S5_W = D_MODEL // 4
S5_P = 16
S5_G = S5_W // S5_P
S5_N = 64
S5_LANES = S5_G * S5_N
GM_W = D_MODEL // 4
GM_H = 4
GM_HD = GM_W // GM_H
CHUNK = 128
N_EXP = 16
CAP_FACTOR = 2
K_END = ATT_W
V_END = K_END + ATT_W
S5_END = V_END + S5_W
Q_END = S5_END + ATT_W
GMLP_END = Q_END + 2 * GM_W
IN_COLS = GMLP_END + 3 * D_MODEL
LOG2E = 1.4426950408889634

V7X_VMEM_BYTES = 64 * 1024 * 1024
VMEM_LIMIT = 56 * 1024 * 1024
FF_CHUNK = 512
COMBINE_GROUP = 8
ATT_PIECE = 256
VT_ROWS = V_DIM + 16
MOD_ROWS = 16


def _params(sem):
    return pltpu.CompilerParams(dimension_semantics=sem, vmem_limit_bytes=VMEM_LIMIT)


def _mod_body(c_ref, w_ref, b_ref, o_ref):
    c = c_ref[...]
    s = (c * jax.nn.sigmoid(c)).astype(BF16)
    o_ref[0] = jnp.dot(s, w_ref[0].astype(BF16), preferred_element_type=F32) + b_ref[0]


def modulation(c_rows, w_mod, b_mod, tn=1536):
    depth, d, n6 = w_mod.shape
    return pl.pallas_call(
        _mod_body,
        grid=(depth, n6 // tn),
        in_specs=[
            pl.BlockSpec((MOD_ROWS, d), lambda l, j: (0, 0)),
            pl.BlockSpec((1, d, tn), lambda l, j: (l, 0, j)),
            pl.BlockSpec((1, 1, tn), lambda l, j: (l, 0, j)),
        ],
        out_specs=pl.BlockSpec((1, MOD_ROWS, tn), lambda l, j: (l, 0, j)),
        out_shape=jax.ShapeDtypeStruct((depth, MOD_ROWS, n6), F32),
        compiler_params=_params(("arbitrary", "arbitrary")),
        name="modulation",
    )(c_rows, w_mod, b_mod.reshape(depth, 1, n6))


def _rope_cols(p, cos, sin):
    lane = lax.broadcasted_iota(jnp.int32, (p.shape[0], 128), 1)
    first = (lane % 32) < ROPE_FREQS
    outs = []
    for s in range(p.shape[1] // 128):
        ps = p[:, s * 128:(s + 1) * 128]
        sw = jnp.where(first, pltpu.roll(ps, 128 - ROPE_FREQS, 1), pltpu.roll(ps, ROPE_FREQS, 1))
        outs.append(ps * cos + sw * sin)
    return jnp.concatenate(outs, axis=1)


def _inproj_body(full, x_ref, mod_ref, ng_ref, w_ref, cos_ref, sin_ref, gg_ref, ws_ref, bs_ref, *outs):
    xt = x_ref[0]
    h = xt * lax.rsqrt(jnp.mean(xt * xt, axis=-1, keepdims=True) + RMS_EPS) * ng_ref[...]
    h = h * (1.0 + mod_ref[0, 1:2, :]) + mod_ref[0, 0:1, :]
    hb = h.astype(BF16)

    def proj(a, b):
        return jnp.dot(hb, w_ref[:, a:b], preferred_element_type=F32)

    cos = cos_ref[...]
    sin = sin_ref[...]
    if full:
        q_ref, k_ref, v_ref, u_ref, ogm_ref, gate_ref = outs
    else:
        k_ref, v_ref, u_ref = outs
    k_ref[0] = _rope_cols(proj(0, K_END), cos, sin).astype(BF16)
    vt = proj(K_END, V_END).T
    pad_row = lax.broadcasted_iota(jnp.int32, (VT_ROWS - V_DIM, vt.shape[1]), 0)
    ones_pad = jnp.where(pad_row == 0, 1.0, 0.0).astype(BF16)
    for hh in range(N_HEADS):
        v_ref[0, hh * VT_ROWS:hh * VT_ROWS + V_DIM, :] = vt[hh * V_DIM:(hh + 1) * V_DIM, :].astype(BF16)
        v_ref[0, hh * VT_ROWS + V_DIM:(hh + 1) * VT_ROWS, :] = ones_pad
    u_ref[0] = proj(V_END, S5_END)
    if not full:
        return
    q_ref[0] = (_rope_cols(proj(S5_END, Q_END), cos, sin) * (HEAD_DIM ** -0.5 * LOG2E)).astype(BF16)

    zg = jax.nn.gelu(proj(Q_END, GMLP_END))
    zu = zg[:, :GM_W]
    zv = zg[:, GM_W:]
    mu = jnp.mean(zv, axis=-1, keepdims=True)
    zc = zv - mu
    zvn = (zc * lax.rsqrt(jnp.mean(zc * zc, axis=-1, keepdims=True) + RMS_EPS) * gg_ref[...]).astype(BF16)
    head = lax.broadcasted_iota(jnp.int32, (CHUNK, GM_W), 1) // GM_HD
    tm = xt.shape[0]
    for c in range(tm // CHUNK):
        zc_blk = zvn[c * CHUNK:(c + 1) * CHUNK, :]
        mixed = bs_ref[...]
        for g in range(GM_H):
            mixed = mixed + jnp.dot(ws_ref[g], jnp.where(head == g, zc_blk, jnp.zeros_like(zc_blk)),
                                    preferred_element_type=F32)
        ogm_ref[0, c * CHUNK:(c + 1) * CHUNK, :] = (zu[c * CHUNK:(c + 1) * CHUNK, :] * mixed).astype(BF16)

    for j in range(3):
        a = GMLP_END + j * D_MODEL
        gate_ref[0, :, j * D_MODEL:(j + 1) * D_MODEL] = jax.nn.sigmoid(proj(a, a + D_MODEL)).astype(BF16)


def input_projection(x, modv, ng, w_in, cos_t, sin_t, gg, ws, bs, full, tm):
    b, n, d = x.shape
    assert n % tm == 0 and tm % CHUNK == 0
    row = lambda bb, j: (bb, j, 0)
    const2 = lambda bb, j: (0, 0)
    out_shape = [jax.ShapeDtypeStruct((b, n, ATT_W), BF16),
                 jax.ShapeDtypeStruct((b, N_HEADS * VT_ROWS, n), BF16),
                 jax.ShapeDtypeStruct((b, n, S5_W), F32)]
    out_specs = [pl.BlockSpec((1, tm, ATT_W), row),
                 pl.BlockSpec((1, N_HEADS * VT_ROWS, tm), lambda bb, j: (bb, 0, j)),
                 pl.BlockSpec((1, tm, S5_W), row)]
    if full:
        out_shape = [jax.ShapeDtypeStruct((b, n, ATT_W), BF16)] + out_shape + [
            jax.ShapeDtypeStruct((b, n, GM_W), BF16), jax.ShapeDtypeStruct((b, n, 3 * d), BF16)]
        out_specs = [pl.BlockSpec((1, tm, ATT_W), row)] + out_specs + [
            pl.BlockSpec((1, tm, GM_W), row), pl.BlockSpec((1, tm, 3 * d), row)]
    return pl.pallas_call(
        functools.partial(_inproj_body, full),
        grid=(b, n // tm),
        in_specs=[
            pl.BlockSpec((1, tm, d), row),
            pl.BlockSpec((1, 6, d), lambda bb, j: (bb, 0, 0)),
            pl.BlockSpec((1, d), const2),
            pl.BlockSpec((d, IN_COLS), const2),
            pl.BlockSpec((tm, 128), lambda bb, j: (j, 0)),
            pl.BlockSpec((tm, 128), lambda bb, j: (j, 0)),
            pl.BlockSpec((1, GM_W), const2),
            pl.BlockSpec((GM_H, CHUNK, CHUNK), lambda bb, j: (0, 0, 0)),
            pl.BlockSpec((CHUNK, GM_W), const2),
        ],
        out_specs=out_specs,
        out_shape=out_shape,
        compiler_params=_params(("arbitrary", "arbitrary")),
        name="input_projection" if full else "input_projection_ctx_side",
    )(x, modv, ng, w_in, cos_t, sin_t, gg, ws, bs)


def _attn_body(lam_init, tk, has_latent, q_ref, kc_ref, vtc_ref, k_ref, vt_ref, dl_ref, sg_ref, o_ref,
               m_ref, acc_ref, q2_ref, s_ref, p_ref, al_ref, bm_ref):
    tq = q_ref.shape[1]
    nc = kc_ref.shape[1]
    n_strip = 2 * tq // 128
    heads = range(q_ref.shape[2] // V_DIM)

    def hl(h):
        return slice(h * V_DIM, (h + 1) * V_DIM)

    def hr(h):
        return slice(h * VT_ROWS, (h + 1) * VT_ROWS)

    for h in heads:
        q = q_ref[0, :, hl(h)]
        lane = lax.broadcasted_iota(jnp.int32, q.shape, 1)
        zero = jnp.zeros_like(q)
        q2_ref[h, 0:tq, :] = jnp.where(lane < HEAD_DIM, q, zero)
        q2_ref[h, tq:2 * tq, :] = jnp.where(lane >= HEAD_DIM, q, zero)
    m_ref[...] = jnp.full(m_ref.shape, -jnp.inf, F32)
    acc_ref[...] = jnp.zeros(acc_ref.shape, F32)

    def qk(h, kb, slot, rows):
        st = lax.dot_general(kb, q2_ref[h], (((1,), (1,)), ((), ())), preferred_element_type=F32)
        for c in range(n_strip):
            s_ref[h, slot, c, 0:rows, :] = st[:, c * 128:(c + 1) * 128]
        bm_ref[h, slot] = jnp.max(st, axis=0, keepdims=True)

    def sm(h, slot, rows):
        m_old = m_ref[h]
        m_new = jnp.maximum(m_old, bm_ref[h, slot])
        al_ref[h, slot] = jnp.exp2(m_old - m_new)
        m_ref[h] = m_new
        piece = min(ATT_PIECE, rows)
        for c in range(n_strip):
            for r0 in range(0, rows, piece):
                p_ref[h, slot, c, r0:r0 + piece, :] = jnp.exp2(
                    s_ref[h, slot, c, r0:r0 + piece, :] - m_new[:, c * 128:(c + 1) * 128]).astype(BF16)

    def pv(h, vtb, slot, rows):
        p = jnp.concatenate([p_ref[h, slot, c, 0:rows, :] for c in range(n_strip)], axis=1)
        acc_ref[h] = al_ref[h, slot] * acc_ref[h] + jnp.dot(vtb, p, preferred_element_type=F32)

    def k_blk(h, j):
        return k_ref[0, pl.ds(pl.multiple_of(j * tk, tk), tk), hl(h)]

    def vt_blk(h, j):
        return vt_ref[0, hr(h), pl.ds(pl.multiple_of(j * tk, tk), tk)]

    for h in heads:
        qk(h, kc_ref[0, :, hl(h)], 0, nc)
        sm(h, 0, nc)
    if not has_latent:
        for h in heads:
            pv(h, vtc_ref[0, hr(h), :], 0, nc)
    else:
        n_kv = k_ref.shape[1] // tk
        assert n_kv >= 4 and n_kv % 2 == 0
        for h in heads:
            qk(h, k_blk(h, 0), 1, tk)
            pv(h, vtc_ref[0, hr(h), :], 0, nc)
            qk(h, k_blk(h, 1), 0, tk)
            sm(h, 1, tk)

        def body(i, carry):
            for h in heads:
                pv(h, vt_blk(h, 2 * i), 1, tk)
                qk(h, k_blk(h, 2 * i + 2), 1, tk)
                sm(h, 0, tk)
                sm(h, 1, tk)
                pv(h, vt_blk(h, 2 * i + 1), 0, tk)
                qk(h, k_blk(h, 2 * i + 3), 0, tk)
            return carry
        lax.fori_loop(0, (n_kv - 2) // 2, body, 0)
        for h in heads:
            pv(h, vt_blk(h, n_kv - 2), 1, tk)
            sm(h, 0, tk)
            pv(h, vt_blk(h, n_kv - 1), 0, tk)

    dl = dl_ref[...]
    lam = (jnp.exp(jnp.sum(dl[0:1, :] * dl[1:2, :], axis=-1, keepdims=True))
           - jnp.exp(jnp.sum(dl[2:3, :] * dl[3:4, :], axis=-1, keepdims=True)) + lam_init)
    for h in heads:
        num = acc_ref[h, 0:V_DIM, :]
        den = acc_ref[h, V_DIM:V_DIM + 1, :]
        ot = num[:, 0:tq] / den[:, 0:tq] - lam * (num[:, tq:2 * tq] / den[:, tq:2 * tq])
        ot = ot * lax.rsqrt(jnp.mean(ot * ot, axis=0, keepdims=True) + RMS_EPS)
        o_ref[0, :, hl(h)] = (ot.T * sg_ref[...] * (1.0 - lam_init)).astype(BF16)


def diff_attention(q, kc, vtc, k, vt, dlam, sg, lam_init, tq, tk, hpb):
    b, nq, _ = q.shape
    nc = kc.shape[1]
    has_latent = k is not None
    if not has_latent:
        k, vt = kc, vtc
    n = k.shape[1]
    assert nq % tq == 0 and (not has_latent or n % tk == 0) and N_HEADS % hpb == 0
    head_rows = lambda bb, h, i: (bb, i, h)
    head_all = lambda bb, h, i: (bb, 0, h)
    head_all_t = lambda bb, h, i: (bb, h, 0)
    kmax = max(tk, nc)
    return pl.pallas_call(
        functools.partial(_attn_body, lam_init, tk, has_latent),
        grid=(b, N_HEADS // hpb, nq // tq),
        in_specs=[
            pl.BlockSpec((1, tq, hpb * V_DIM), head_rows),
            pl.BlockSpec((1, nc, hpb * V_DIM), head_all),
            pl.BlockSpec((1, hpb * VT_ROWS, nc), head_all_t),
            pl.BlockSpec((1, n, hpb * V_DIM), head_all),
            pl.BlockSpec((1, hpb * VT_ROWS, n), head_all_t),
            pl.BlockSpec((4, HEAD_DIM), lambda bb, h, i: (0, 0)),
            pl.BlockSpec((1, V_DIM), lambda bb, h, i: (0, 0)),
        ],
        out_specs=pl.BlockSpec((1, tq, hpb * V_DIM), head_rows),
        out_shape=jax.ShapeDtypeStruct((b, nq, ATT_W), BF16),
        scratch_shapes=[pltpu.VMEM((hpb, 1, 2 * tq), F32), pltpu.VMEM((hpb, VT_ROWS, 2 * tq), F32),
                        pltpu.VMEM((hpb, 2 * tq, V_DIM), BF16),
                        pltpu.VMEM((hpb, 2, 2 * tq // 128, kmax, 128), F32),
                        pltpu.VMEM((hpb, 2, 2 * tq // 128, kmax, 128), BF16),
                        pltpu.VMEM((hpb, 2, 1, 2 * tq), F32), pltpu.VMEM((hpb, 2, 1, 2 * tq), F32)],
        compiler_params=_params(("arbitrary", "arbitrary", "arbitrary")),
        name="diff_attention" if has_latent else "diff_attention_ctx",
    )(q, kc, vtc, k, vt, dlam, sg)


def _s5_body(t_chunk, u_ref, wb_ref, a_ref, wc_ref, dsk_ref, y_ref, xs_ref, st_ref):
    d = pl.program_id(0)
    kk = pl.program_id(1)
    nb = u_ref.shape[1]

    @pl.when(kk == 0)
    def _():
        st_ref[...] = jnp.zeros(st_ref.shape, F32)

    u = u_ref[...].reshape(t_chunk * nb, S5_W)
    xs_ref[...] = jnp.dot(u.astype(BF16), wb_ref[0], preferred_element_type=F32).reshape(
        t_chunk, nb, 2 * S5_LANES)
    ar = jnp.broadcast_to(a_ref[0, 0:1, :], (nb, S5_LANES))
    ai = jnp.broadcast_to(a_ref[0, 1:2, :], (nb, S5_LANES))

    def body(i, carry):
        sr, si = carry
        t = jnp.where(d == 0, i, t_chunk - 1 - i)
        xr = xs_ref[t, :, 0:S5_LANES]
        xi = xs_ref[t, :, S5_LANES:2 * S5_LANES]
        nr = ar * sr - ai * si + xr
        ni = ar * si + ai * sr + xi
        xs_ref[t, :, 0:S5_LANES] = nr
        xs_ref[t, :, S5_LANES:2 * S5_LANES] = ni
        return nr, ni

    sr, si = lax.fori_loop(0, t_chunk, body, (st_ref[0], st_ref[1]), unroll=4)
    st_ref[0] = sr
    st_ref[1] = si
    y = jnp.dot(xs_ref[...].reshape(t_chunk * nb, 2 * S5_LANES).astype(BF16), wc_ref[0],
                preferred_element_type=F32)
    y = y + jnp.where(d == 0, 1.0, 0.0) * (dsk_ref[...] * u)
    y_ref[0] = y.reshape(t_chunk, nb, S5_W)


def s5_scan(u_all, wb, a, wc, dskip, n_ctx, t_chunk):
    tt, nb, _ = u_all.shape
    assert n_ctx % t_chunk == 0 and tt % t_chunk == 0
    nchunk = tt // t_chunk
    cctx = n_ctx // t_chunk

    def chunk_index(d, kk):
        bwd = jnp.where(kk < cctx, cctx - 1 - kk, nchunk - 1 - (kk - cctx))
        return jnp.where(d == 0, kk, bwd)

    return pl.pallas_call(
        functools.partial(_s5_body, t_chunk),
        grid=(2, nchunk),
        in_specs=[
            pl.BlockSpec((t_chunk, nb, S5_W), lambda d, kk: (chunk_index(d, kk), 0, 0)),
            pl.BlockSpec((1, S5_W, 2 * S5_LANES), lambda d, kk: (d, 0, 0)),
            pl.BlockSpec((1, 2, S5_LANES), lambda d, kk: (d, 0, 0)),
            pl.BlockSpec((1, 2 * S5_LANES, S5_W), lambda d, kk: (d, 0, 0)),
            pl.BlockSpec((1, S5_W), lambda d, kk: (0, 0)),
        ],
        out_specs=pl.BlockSpec((1, t_chunk, nb, S5_W), lambda d, kk: (d, chunk_index(d, kk), 0, 0)),
        out_shape=jax.ShapeDtypeStruct((2, tt, nb, S5_W), F32),
        scratch_shapes=[pltpu.VMEM((t_chunk, nb, 2 * S5_LANES), F32), pltpu.VMEM((2, nb, S5_LANES), F32)],
        compiler_params=_params(("arbitrary", "arbitrary")),
        name="s5_scan",
    )(u_all, wb, a, wc, dskip)


def s5_weights(lam_re, lam_im, log_dt, b_re, b_im, c_re, c_im):
    dt = jnp.exp(log_dt.astype(F32))[..., None]
    lr, li = lam_re.astype(F32), lam_im.astype(F32)
    mag = jnp.exp(lr * dt)
    ar, ai = mag * jnp.cos(li * dt), mag * jnp.sin(li * dt)
    den = lr * lr + li * li
    fr = ((ar - 1.0) * lr + ai * li) / den
    fi = (ai * lr - (ar - 1.0) * li) / den
    bbr = fr[..., None] * b_re - fi[..., None] * b_im
    bbi = fr[..., None] * b_im + fi[..., None] * b_re
    eye = jnp.eye(S5_G, dtype=F32)
    def in_map(bb):
        return jnp.einsum('dgnc,gh->dgchn', bb, eye).reshape(2, S5_W, S5_LANES)
    wb = jnp.concatenate([in_map(bbr), in_map(bbi)], axis=-1).astype(BF16)
    def out_map(cc):
        return jnp.einsum('dgcn,gh->dgnhc', cc, eye).reshape(2, S5_LANES, S5_W)
    wc = jnp.concatenate([out_map(c_re.astype(F32)), out_map(-c_im.astype(F32))], axis=1).astype(BF16)
    a = jnp.stack([ar.reshape(2, S5_LANES), ai.reshape(2, S5_LANES)], axis=1)
    return wb, a, wc


def _merge_body(x_ref, oatt_ref, yf_ref, yb_ref, ogm_ref, gate_ref, mod_ref, ng_ref, watt_ref, wglu_ref,
                wgm_ref, wout_ref, wr_ref, xo_ref, h2_ref, pr_ref):
    y_att = jnp.dot(oatt_ref[0], watt_ref[...], preferred_element_type=F32)
    y5 = yf_ref[0] + yb_ref[0]
    z = jnp.dot(jax.nn.gelu(y5).astype(BF16), wglu_ref[...], preferred_element_type=F32)
    y5g = z[:, :D_MODEL] * jax.nn.sigmoid(z[:, D_MODEL:])
    y_gm = jnp.dot(ogm_ref[0], wgm_ref[...], preferred_element_type=F32)
    merged = (gate_ref[0, :, 0:D_MODEL].astype(F32) * y_att
              + gate_ref[0, :, D_MODEL:2 * D_MODEL].astype(F32) * y5g
              + gate_ref[0, :, 2 * D_MODEL:3 * D_MODEL].astype(F32) * y_gm)
    mix = jnp.dot(merged.astype(BF16), wout_ref[...], preferred_element_type=F32)
    mixn = mix * lax.rsqrt(jnp.mean(mix * mix, axis=-1, keepdims=True) + RMS_EPS) * ng_ref[0:1, :]
    xn = x_ref[0] + mod_ref[0, 2:3, :] * mixn
    xo_ref[0] = xn
    h2 = xn * lax.rsqrt(jnp.mean(xn * xn, axis=-1, keepdims=True) + RMS_EPS) * ng_ref[1:2, :]
    h2 = h2 * (1.0 + mod_ref[0, 4:5, :]) + mod_ref[0, 3:4, :]
    h2_ref[0] = h2
    logits = lax.dot_general(wr_ref[...], h2.astype(BF16), (((1,), (1,)), ((), ())),
                             preferred_element_type=F32)
    pe = jnp.exp(logits - jnp.max(logits, axis=0, keepdims=True))
    pr_ref[0] = pe / jnp.sum(pe, axis=0, keepdims=True)


def merge_branches(x, o_att, yf, yb, ogm, gates, modv, ng12, watt, wglu, wgm, wout, wr_t, tm):
    b, n, d = x.shape
    row = lambda bb, j: (bb, j, 0)
    const2 = lambda bb, j: (0, 0)
    return pl.pallas_call(
        _merge_body,
        grid=(b, n // tm),
        in_specs=[
            pl.BlockSpec((1, tm, d), row),
            pl.BlockSpec((1, tm, ATT_W), row),
            pl.BlockSpec((1, tm, S5_W), row),
            pl.BlockSpec((1, tm, S5_W), row),
            pl.BlockSpec((1, tm, GM_W), row),
            pl.BlockSpec((1, tm, 3 * d), row),
            pl.BlockSpec((1, 6, d), lambda bb, j: (bb, 0, 0)),
            pl.BlockSpec((2, d), const2),
            pl.BlockSpec((ATT_W, d), const2),
            pl.BlockSpec((S5_W, 2 * d), const2),
            pl.BlockSpec((GM_W, d), const2),
            pl.BlockSpec((d, d), const2),
            pl.BlockSpec((N_EXP, d), const2),
        ],
        out_specs=[pl.BlockSpec((1, tm, d), row), pl.BlockSpec((1, tm, d), row),
                   pl.BlockSpec((1, N_EXP, tm), lambda bb, j: (bb, 0, j))],
        out_shape=[jax.ShapeDtypeStruct((b, n, d), F32), jax.ShapeDtypeStruct((b, n, d), F32),
                   jax.ShapeDtypeStruct((b, N_EXP, n), F32)],
        compiler_params=_params(("arbitrary", "arbitrary")),
        name="merge_branches",
    )(x, o_att, yf, yb, ogm, gates, modv, ng12, watt, wglu, wgm, wout, wr_t)


def _cumsum_lanes(x, tri):
    outs = []
    run = jnp.zeros((x.shape[0], 1), F32)
    for c in range(x.shape[1] // 128):
        blk = jnp.dot(x[:, c * 128:(c + 1) * 128].astype(BF16), tri, preferred_element_type=F32) + run
        outs.append(blk)
        run = blk[:, 127:128]
    return jnp.concatenate(outs, axis=1)


def _topk_body(cap, p_ref, idx_ref, cs_ref):
    p = p_ref[0]
    ne, n = p.shape
    bits = pltpu.bitcast(p, jnp.int32)
    thr = jnp.zeros((ne, 1), jnp.int32)
    for bit in range(30, -1, -1):
        cand = thr | (1 << bit)
        cnt = jnp.sum(jnp.where(bits >= cand, 1.0, 0.0), axis=-1, keepdims=True)
        thr = jnp.where(cnt >= cap, cand, thr)
    gt = bits > thr
    eq = bits == thr
    n_gt = jnp.sum(jnp.where(gt, 1.0, 0.0), axis=-1, keepdims=True)
    r = lax.broadcasted_iota(jnp.int32, (128, 128), 0)
    c = lax.broadcasted_iota(jnp.int32, (128, 128), 1)
    tri = jnp.where(r <= c, 1.0, 0.0).astype(BF16)
    eq_f = jnp.where(eq, 1.0, 0.0)
    eq_rank = _cumsum_lanes(eq_f, tri) - eq_f
    sel = jnp.where(gt | (eq & (eq_rank < cap - n_gt)), 1.0, 0.0)
    cs_ref[...] = _cumsum_lanes(sel, tri)
    nblk = idx_ref.shape[2] // 128

    def per_expert(e, carry):
        row = cs_ref[pl.ds(e, 1), :]
        blks = []
        for sb in range(nblk):
            slot = (lax.broadcasted_iota(jnp.int32, (128, 1), 0) + sb * 128).astype(F32)
            cnt = jnp.sum(jnp.where(row <= slot, 1.0, 0.0), axis=-1, keepdims=True)
            blks.append(jnp.broadcast_to(cnt, (128, 128)).T[0:1, :])
        idx_ref[0, pl.ds(e, 1), :] = jnp.concatenate(blks, axis=1).astype(jnp.int32)
        return carry
    lax.fori_loop(0, ne, per_expert, 0)


def expert_topk(probs_t, cap):
    b, ne, n = probs_t.shape
    cap_pad = -(-cap // 128) * 128
    return pl.pallas_call(
        functools.partial(_topk_body, cap),
        grid=(b,),
        in_specs=[pl.BlockSpec((1, ne, n), lambda bb: (bb, 0, 0))],
        out_specs=pl.BlockSpec((1, ne, cap_pad), lambda bb: (bb, 0, 0)),
        out_shape=jax.ShapeDtypeStruct((b, ne, cap_pad), jnp.int32),
        scratch_shapes=[pltpu.VMEM((ne, n), F32)],
        compiler_params=_params(("arbitrary",)),
        name="expert_topk",
    )(probs_t)


def _moe_body(cap, nbp, idxa_ref, idxb_ref, idxn_ref, h2_hbm, wr_ref, wg_ref, wu_ref, wd_ref, ye_ref,
              xa_ref, xb_ref, sem):
    e = pl.program_id(0)
    bp = pl.program_id(1)
    step = e * nbp + bp
    last = pl.num_programs(0) * nbp - 1
    bp_next = jnp.minimum(step + 1, last) % nbp

    def row_copy(idx_smem, bb, buf, s, g, r):
        t = idx_smem[0, 0, g * 8 + r]
        return pltpu.make_async_copy(h2_hbm.at[bb, pl.ds(t, 1), :], buf.at[g, pl.ds(r, 1), :], sem.at[s])

    def issue_rolled(idx_smem, bb, buf, s):
        def body(g, carry):
            for r in range(8):
                row_copy(idx_smem, bb, buf, s, g, r).start()
            return carry
        lax.fori_loop(0, cap // 8, body, 0)

    def issue_inline(idx_smem, bb, buf, s):
        for g in range(cap // 8):
            for r in range(8):
                row_copy(idx_smem, bb, buf, s, g, r).start()

    def wait_all(buf, s):
        pltpu.make_async_copy(buf, buf, sem.at[s]).wait()

    def compute(buf, out_i):
        xe = buf[...].reshape(cap, buf.shape[-1]).astype(BF16)
        logits = jnp.dot(xe, wr_ref[...], preferred_element_type=F32)
        pe = jnp.exp(logits - jnp.max(logits, axis=-1, keepdims=True))
        probs = pe / jnp.sum(pe, axis=-1, keepdims=True)
        col = lax.broadcasted_iota(jnp.int32, probs.shape, 1)
        gate = jnp.sum(jnp.where(col == e, probs, 0.0), axis=-1, keepdims=True)
        ye = jnp.zeros((cap, xe.shape[1]), F32)
        for f0 in range(0, wg_ref.shape[2], FF_CHUNK):
            hg = jnp.dot(xe, wg_ref[0, :, f0:f0 + FF_CHUNK], preferred_element_type=F32)
            hu = jnp.dot(xe, wu_ref[0, :, f0:f0 + FF_CHUNK], preferred_element_type=F32)
            hid = (hg * jax.nn.sigmoid(hg) * hu).astype(BF16)
            ye = ye + jnp.dot(hid, wd_ref[0, f0:f0 + FF_CHUNK, :], preferred_element_type=F32)
        ye_ref[out_i, 0] = ye * gate

    @pl.when(step == 0)
    def _():
        issue_rolled(idxa_ref, 2 * bp, xa_ref, 0)

    wait_all(xa_ref, 0)
    issue_inline(idxb_ref, 2 * bp + 1, xb_ref, 1)
    compute(xa_ref, 0)
    wait_all(xb_ref, 1)
    issue_inline(idxn_ref, 2 * bp_next, xa_ref, 0)
    compute(xb_ref, 1)

    @pl.when(step == last)
    def _():
        wait_all(xa_ref, 0)


def moe_experts(h2, idx, cap, wr, wg, wu, wd):
    b, n, d = h2.shape
    ne, cap_pad = idx.shape[1:]
    ff = wg.shape[-1]
    assert b % 2 == 0 and cap % 8 == 0
    nbp = b // 2
    idx3 = idx.reshape(b * ne, 1, cap_pad)

    def idx_a(e, bp):
        return (2 * bp * ne + e, 0, 0)

    def idx_b(e, bp):
        return ((2 * bp + 1) * ne + e, 0, 0)

    def idx_next(e, bp):
        step = jnp.minimum(e * nbp + bp + 1, ne * nbp - 1)
        return (2 * (step % nbp) * ne + step // nbp, 0, 0)

    once = pl.Buffered(1)
    return pl.pallas_call(
        functools.partial(_moe_body, cap, nbp),
        grid=(ne, nbp),
        in_specs=[
            pl.BlockSpec((1, 1, cap_pad), idx_a, memory_space=pltpu.SMEM),
            pl.BlockSpec((1, 1, cap_pad), idx_b, memory_space=pltpu.SMEM),
            pl.BlockSpec((1, 1, cap_pad), idx_next, memory_space=pltpu.SMEM),
            pl.BlockSpec(memory_space=pl.ANY),
            pl.BlockSpec((d, ne), lambda e, bp: (0, 0)),
            pl.BlockSpec((1, d, ff), lambda e, bp: (e, 0, 0), pipeline_mode=once),
            pl.BlockSpec((1, d, ff), lambda e, bp: (e, 0, 0), pipeline_mode=once),
            pl.BlockSpec((1, ff, d), lambda e, bp: (e, 0, 0), pipeline_mode=once),
        ],
        out_specs=pl.BlockSpec((2, 1, cap, d), lambda e, bp: (bp, e, 0, 0)),
        out_shape=jax.ShapeDtypeStruct((b, ne, cap, d), F32),
        scratch_shapes=[pltpu.VMEM((cap // 8, 8, d), F32), pltpu.VMEM((cap // 8, 8, d), F32),
                        pltpu.SemaphoreType.DMA((2,))],
        compiler_params=_params(("arbitrary", "arbitrary")),
        name="moe_experts",
    )(idx3, idx3, idx3, h2, wr, wg, wu, wd)


def _combine_body(cap, idx_ref, ye_ref, acc_ref):
    e = pl.program_id(2)

    @pl.when(e == 0)
    def _():
        acc_ref[...] = jnp.zeros(acc_ref.shape, F32)

    def body(g, carry):
        ts = [idx_ref[0, 0, g * COMBINE_GROUP + r] for r in range(COMBINE_GROUP)]
        rows = [acc_ref[0, pl.ds(ts[r], 1), :] + ye_ref[0, 0, pl.ds(g * COMBINE_GROUP + r, 1), :]
                for r in range(COMBINE_GROUP)]
        for r in range(COMBINE_GROUP):
            acc_ref[0, pl.ds(ts[r], 1), :] = rows[r]
        return carry
    lax.fori_loop(0, cap // COMBINE_GROUP, body, 0)


def moe_combine(ye, idx, n, dh=512):
    b, ne, cap, d = ye.shape
    cap_pad = idx.shape[2]
    idx3 = idx.reshape(b * ne, 1, cap_pad)
    return pl.pallas_call(
        functools.partial(_combine_body, cap),
        grid=(b, d // dh, ne),
        in_specs=[
            pl.BlockSpec((1, 1, cap_pad), lambda bb, j, e: (bb * ne + e, 0, 0), memory_space=pltpu.SMEM),
            pl.BlockSpec((1, 1, cap, dh), lambda bb, j, e: (bb, e, 0, j)),
        ],
        out_specs=pl.BlockSpec((1, n, dh), lambda bb, j, e: (bb, 0, j)),
        out_shape=jax.ShapeDtypeStruct((b, n, d), F32),
        compiler_params=_params(("arbitrary", "arbitrary", "arbitrary")),
        name="moe_combine",
    )(idx3, ye)


def _resid_body(x_ref, y_ref, mod_ref, ng_ref, o_ref):
    y = y_ref[0]
    yn = y * lax.rsqrt(jnp.mean(y * y, axis=-1, keepdims=True) + RMS_EPS) * ng_ref[...]
    o_ref[0] = x_ref[0] + mod_ref[0, 5:6, :] * yn


def moe_residual(x, y, modv, ng3, tm):
    b, n, d = x.shape
    row = lambda bb, j: (bb, j, 0)
    return pl.pallas_call(
        _resid_body,
        grid=(b, n // tm),
        in_specs=[pl.BlockSpec((1, tm, d), row), pl.BlockSpec((1, tm, d), row),
                  pl.BlockSpec((1, 6, d), lambda bb, j: (bb, 0, 0)), pl.BlockSpec((1, d), lambda bb, j: (0, 0))],
        out_specs=pl.BlockSpec((1, tm, d), row),
        out_shape=jax.ShapeDtypeStruct((b, n, d), F32),
        compiler_params=_params(("arbitrary", "arbitrary")),
        name="moe_residual",
    )(x, y, modv, ng3)


def _rope_tables(n):
    rows = n // GRID_W
    row = jnp.repeat(jnp.arange(rows), GRID_W).astype(F32)
    col = jnp.tile(jnp.arange(GRID_W), rows).astype(F32)
    inv = ROPE_THETA ** (-jnp.arange(ROPE_FREQS, dtype=F32) / ROPE_FREQS)
    ar, ac = row[:, None] * inv, col[:, None] * inv
    cos64 = jnp.concatenate([jnp.cos(ar), jnp.cos(ar), jnp.cos(ac), jnp.cos(ac)], axis=1)
    sin64 = jnp.concatenate([-jnp.sin(ar), jnp.sin(ar), -jnp.sin(ac), jnp.sin(ac)], axis=1)
    return jnp.tile(cos64, (1, 2)), jnp.tile(sin64, (1, 2))


def _moe_sublayer(x_mid, h2, probs_t, modv, ng3, wr, wg, wu, wd, tm):
    n = x_mid.shape[1]
    cap = CAP_FACTOR * n // N_EXP
    idx = expert_topk(probs_t, cap)
    ye = moe_experts(h2, idx, cap, wr, wg, wu, wd)
    y = moe_combine(ye, idx, n)
    return moe_residual(x_mid, y, modv, ng3, tm)


def _forward(x, c, ctx, c_ctx, w_mod, b_mod, norm_g, w_in, diff_lambda, diff_subln_g, s5_lam_re, s5_lam_im,
             s5_log_dt, s5_b_re, s5_b_im, s5_c_re, s5_c_im, s5_d, w_s5_glu, gmlp_norm_g, gmlp_ws, gmlp_bs,
             w_br_att, w_br_gmlp, w_out, w_router, w_e_gate, w_e_up, w_e_down, *, tm, tq, tk, t_chunk, hpb):
    b, n, d = x.shape
    n_ctx = ctx.shape[1]
    depth = w_mod.shape[0]
    tm_c = min(tm, n_ctx)
    tq_c = min(tq, n_ctx)

    c_rows = jnp.zeros((MOD_ROWS, d), F32).at[:b].set(c).at[b].set(c_ctx)
    mod_all = modulation(c_rows, w_mod, b_mod).reshape(depth, MOD_ROWS, 6, d)
    cos_t, sin_t = _rope_tables(n)
    cos_c, sin_c = jnp.ones((n_ctx, 128), F32), jnp.zeros((n_ctx, 128), F32)

    xc = ctx
    for l in range(depth):
        last = l == depth - 1
        lam_init = 0.8 - 0.6 * math.exp(-0.3 * l)
        modv = mod_all[l, :b]
        modc = jnp.broadcast_to(mod_all[l, b:b + 1], (b, 6, d))
        ng0 = norm_g[l, 0:1]
        w_in_b = w_in[l].astype(BF16)
        gg = gmlp_norm_g[l][None, :]
        ws = gmlp_ws[l].astype(BF16)
        bs = jnp.repeat(gmlp_bs[l].T, GM_HD, axis=1)
        sg = diff_subln_g[l][None, :]
        wr = w_router[l].astype(BF16)
        wg, wu, wd = w_e_gate[l].astype(BF16), w_e_up[l].astype(BF16), w_e_down[l].astype(BF16)

        q, k, v, u, ogm, gates = input_projection(x, modv, ng0, w_in_b, cos_t, sin_t, gg, ws, bs, True, tm)
        pc = input_projection(xc, modc, ng0, w_in_b, cos_c, sin_c, gg, ws, bs, not last, tm_c)
        if last:
            kc, vc, uc = pc
        else:
            qc, kc, vc, uc, ogm_c, gates_c = pc

        o_att = diff_attention(q, kc, vc, k, v, diff_lambda[l], sg, lam_init, tq, tk, hpb)

        wb, a, wc = s5_weights(s5_lam_re[l], s5_lam_im[l], s5_log_dt[l], s5_b_re[l], s5_b_im[l],
                               s5_c_re[l], s5_c_im[l])
        u_all = jnp.concatenate([uc, u], axis=1).transpose(1, 0, 2)
        y5 = s5_scan(u_all, wb, a, wc, s5_d[l][None, :], n_ctx, min(t_chunk, n_ctx)).transpose(0, 2, 1, 3)

        merge_w = (norm_g[l, 1:3], w_br_att[l].astype(BF16), w_s5_glu[l].astype(BF16),
                   w_br_gmlp[l].astype(BF16), w_out[l].astype(BF16), w_router[l].T.astype(BF16))
        x_mid, h2, probs_t = merge_branches(x, o_att, y5[0, :, n_ctx:], y5[1, :, n_ctx:], ogm, gates, modv,
                                            *merge_w, tm)
        x = _moe_sublayer(x_mid, h2, probs_t, modv, norm_g[l, 3:4], wr, wg, wu, wd, tm)
        if not last:
            o_att_c = diff_attention(qc, kc, vc, None, None, diff_lambda[l], sg, lam_init, tq_c, tk, hpb)
            xc_mid, h2c, probs_c = merge_branches(xc, o_att_c, y5[0, :, :n_ctx], y5[1, :, :n_ctx], ogm_c,
                                                  gates_c, modc, *merge_w, tm_c)
            xc = _moe_sublayer(xc_mid, h2c, probs_c, modc, norm_g[l, 3:4], wr, wg, wu, wd, tm_c)
    return x


def kernel(x, c, ctx, c_ctx, w_mod, b_mod, norm_g, w_in, diff_lambda, diff_subln_g, s5_lam_re, s5_lam_im, s5_log_dt, s5_b_re, s5_b_im, s5_c_re, s5_c_im, s5_d, w_s5_glu, gmlp_norm_g, gmlp_ws, gmlp_bs, w_br_att, w_br_gmlp, w_out, w_router, w_e_gate, w_e_up, w_e_down):
    return _forward(x, c, ctx, c_ctx, w_mod, b_mod, norm_g, w_in, diff_lambda, diff_subln_g, s5_lam_re, s5_lam_im,
                    s5_log_dt, s5_b_re, s5_b_im, s5_c_re, s5_c_im, s5_d, w_s5_glu, gmlp_norm_g, gmlp_ws, gmlp_bs,
                    w_br_att, w_br_gmlp, w_out, w_router, w_e_gate, w_e_up, w_e_down,
                    tm=512, tq=512, tk=512, t_chunk=256, hpb=2)
```

```python
import functools
import math

import jax
import jax.numpy as jnp
from jax import lax
from jax.experimental import pallas as pl
from jax.experimental.pallas import tpu as pltpu

F32 = jnp.float32
BF16 = jnp.bfloat16

D_MODEL = 1024
GRID_W = 64
RMS_EPS = 1e-6
N_HEADS = 4
HEAD_DIM = 64
V_DIM = 2 * HEAD_DIM
ATT_W = N_HEADS * V_DIM
ROPE_THETA = 10000.0
ROPE_FREQS = HEAD_DIM // 4
S5_W = D_MODEL // 4
S5_P = 16
S5_G = S5_W // S5_P
S5_N = 64
S5_LANES = S5_G * S5_N
GM_W = D_MODEL // 4
GM_H = 4
GM_HD = GM_W // GM_H
CHUNK = 128
N_EXP = 16
CAP_FACTOR = 2
K_END = ATT_W
V_END = K_END + ATT_W
S5_END = V_END + S5_W
Q_END = S5_END + ATT_W
GMLP_END = Q_END + 2 * GM_W
IN_COLS = GMLP_END + 3 * D_MODEL
LOG2E = 1.4426950408889634

V7X_VMEM_BYTES = 64 * 1024 * 1024
VMEM_LIMIT = 56 * 1024 * 1024
FF_CHUNK = 512
S5_PARTS = 4
COMBINE_GROUP = 4
ATT_PIECE = 256
VT_ROWS = V_DIM + 16
MOD_ROWS = 16


def _params(sem):
    return pltpu.CompilerParams(dimension_semantics=sem, vmem_limit_bytes=VMEM_LIMIT)


def _mod_body(c_ref, w_ref, b_ref, o_ref):
    c = c_ref[...]
    s = (c * jax.nn.sigmoid(c)).astype(BF16)
    o_ref[0] = jnp.dot(s, w_ref[0].astype(BF16), preferred_element_type=F32) + b_ref[0]


def modulation(c_rows, w_mod, b_mod, tn=1536):
    depth, d, n6 = w_mod.shape
    return pl.pallas_call(
        _mod_body,
        grid=(depth, n6 // tn),
        in_specs=[
            pl.BlockSpec((MOD_ROWS, d), lambda l, j: (0, 0)),
            pl.BlockSpec((1, d, tn), lambda l, j: (l, 0, j)),
            pl.BlockSpec((1, 1, tn), lambda l, j: (l, 0, j)),
        ],
        out_specs=pl.BlockSpec((1, MOD_ROWS, tn), lambda l, j: (l, 0, j)),
        out_shape=jax.ShapeDtypeStruct((depth, MOD_ROWS, n6), F32),
        compiler_params=_params(("arbitrary", "arbitrary")),
        name="modulation",
    )(c_rows, w_mod, b_mod.reshape(depth, 1, n6))


def _rope_cols(p, cos, sin):
    lane = lax.broadcasted_iota(jnp.int32, (p.shape[0], 128), 1)
    first = (lane % 32) < ROPE_FREQS
    outs = []
    for s in range(p.shape[1] // 128):
        ps = p[:, s * 128:(s + 1) * 128]
        sw = jnp.where(first, pltpu.roll(ps, 128 - ROPE_FREQS, 1), pltpu.roll(ps, ROPE_FREQS, 1))
        outs.append(ps * cos + sw * sin)
    return jnp.concatenate(outs, axis=1)


def _inproj_body(full, x_ref, mod_ref, ng_ref, w_ref, cos_ref, sin_ref, gg_ref, ws_ref, bs_ref, *outs):
    xt = x_ref[0]
    h = xt * lax.rsqrt(jnp.mean(xt * xt, axis=-1, keepdims=True) + RMS_EPS) * ng_ref[...]
    h = h * (1.0 + mod_ref[0, 1:2, :]) + mod_ref[0, 0:1, :]
    hb = h.astype(BF16)

    def proj(a, b):
        return jnp.dot(hb, w_ref[:, a:b], preferred_element_type=F32)

    cos = cos_ref[...]
    sin = sin_ref[...]
    if full:
        q_ref, k_ref, v_ref, u_ref, ogm_ref, gate_ref = outs
    else:
        k_ref, v_ref, u_ref = outs
    k_ref[0] = _rope_cols(proj(0, K_END), cos, sin).astype(BF16)
    vt = proj(K_END, V_END).T
    pad_row = lax.broadcasted_iota(jnp.int32, (VT_ROWS - V_DIM, vt.shape[1]), 0)
    ones_pad = jnp.where(pad_row == 0, 1.0, 0.0).astype(BF16)
    for hh in range(N_HEADS):
        v_ref[0, hh * VT_ROWS:hh * VT_ROWS + V_DIM, :] = vt[hh * V_DIM:(hh + 1) * V_DIM, :].astype(BF16)
        v_ref[0, hh * VT_ROWS + V_DIM:(hh + 1) * VT_ROWS, :] = ones_pad
    u_ref[0] = proj(V_END, S5_END)
    if not full:
        return
    q_ref[0] = (_rope_cols(proj(S5_END, Q_END), cos, sin) * (HEAD_DIM ** -0.5 * LOG2E)).astype(BF16)

    zg = jax.nn.gelu(proj(Q_END, GMLP_END))
    zu = zg[:, :GM_W]
    zv = zg[:, GM_W:]
    mu = jnp.mean(zv, axis=-1, keepdims=True)
    zc = zv - mu
    zvn = (zc * lax.rsqrt(jnp.mean(zc * zc, axis=-1, keepdims=True) + RMS_EPS) * gg_ref[...]).astype(BF16)
    head = lax.broadcasted_iota(jnp.int32, (CHUNK, GM_W), 1) // GM_HD
    tm = xt.shape[0]
    for c in range(tm // CHUNK):
        zc_blk = zvn[c * CHUNK:(c + 1) * CHUNK, :]
        mixed = bs_ref[...]
        for g in range(GM_H):
            mixed = mixed + jnp.dot(ws_ref[g], jnp.where(head == g, zc_blk, jnp.zeros_like(zc_blk)),
                                    preferred_element_type=F32)
        ogm_ref[0, c * CHUNK:(c + 1) * CHUNK, :] = (zu[c * CHUNK:(c + 1) * CHUNK, :] * mixed).astype(BF16)

    for j in range(3):
        a = GMLP_END + j * D_MODEL
        gate_ref[0, :, j * D_MODEL:(j + 1) * D_MODEL] = jax.nn.sigmoid(proj(a, a + D_MODEL)).astype(BF16)


def input_projection(x, modv, ng, w_in, cos_t, sin_t, gg, ws, bs, full, tm):
    b, n, d = x.shape
    assert n % tm == 0 and tm % CHUNK == 0
    row = lambda bb, j: (bb, j, 0)
    const2 = lambda bb, j: (0, 0)
    out_shape = [jax.ShapeDtypeStruct((b, n, ATT_W), BF16),
                 jax.ShapeDtypeStruct((b, N_HEADS * VT_ROWS, n), BF16),
                 jax.ShapeDtypeStruct((b, n, S5_W), F32)]
    out_specs = [pl.BlockSpec((1, tm, ATT_W), row),
                 pl.BlockSpec((1, N_HEADS * VT_ROWS, tm), lambda bb, j: (bb, 0, j)),
                 pl.BlockSpec((1, tm, S5_W), row)]
    if full:
        out_shape = [jax.ShapeDtypeStruct((b, n, ATT_W), BF16)] + out_shape + [
            jax.ShapeDtypeStruct((b, n, GM_W), BF16), jax.ShapeDtypeStruct((b, n, 3 * d), BF16)]
        out_specs = [pl.BlockSpec((1, tm, ATT_W), row)] + out_specs + [
            pl.BlockSpec((1, tm, GM_W), row), pl.BlockSpec((1, tm, 3 * d), row)]
    return pl.pallas_call(
        functools.partial(_inproj_body, full),
        grid=(b, n // tm),
        in_specs=[
            pl.BlockSpec((1, tm, d), row),
            pl.BlockSpec((1, 6, d), lambda bb, j: (bb, 0, 0)),
            pl.BlockSpec((1, d), const2),
            pl.BlockSpec((d, IN_COLS), const2),
            pl.BlockSpec((tm, 128), lambda bb, j: (j, 0)),
            pl.BlockSpec((tm, 128), lambda bb, j: (j, 0)),
            pl.BlockSpec((1, GM_W), const2),
            pl.BlockSpec((GM_H, CHUNK, CHUNK), lambda bb, j: (0, 0, 0)),
            pl.BlockSpec((CHUNK, GM_W), const2),
        ],
        out_specs=out_specs,
        out_shape=out_shape,
        compiler_params=_params(("arbitrary", "arbitrary")),
        name="input_projection" if full else "input_projection_ctx_side",
    )(x, modv, ng, w_in, cos_t, sin_t, gg, ws, bs)


def _attn_body(lam_init, tk, has_latent, q_ref, kc_ref, vtc_ref, k_ref, vt_ref, dl_ref, sg_ref, o_ref,
               m_ref, acc_ref, q2_ref, s_ref, p_ref, al_ref, bm_ref):
    tq = q_ref.shape[1]
    nc = kc_ref.shape[1]
    n_strip = 2 * tq // 128
    heads = range(q_ref.shape[2] // V_DIM)

    def hl(h):
        return slice(h * V_DIM, (h + 1) * V_DIM)

    def hr(h):
        return slice(h * VT_ROWS, (h + 1) * VT_ROWS)

    for h in heads:
        q = q_ref[0, :, hl(h)]
        lane = lax.broadcasted_iota(jnp.int32, q.shape, 1)
        zero = jnp.zeros_like(q)
        q2_ref[h, 0:tq, :] = jnp.where(lane < HEAD_DIM, q, zero)
        q2_ref[h, tq:2 * tq, :] = jnp.where(lane >= HEAD_DIM, q, zero)
    m_ref[...] = jnp.full(m_ref.shape, -jnp.inf, F32)
    acc_ref[...] = jnp.zeros(acc_ref.shape, F32)

    def qk(h, kb, slot, rows):
        st = lax.dot_general(kb, q2_ref[h], (((1,), (1,)), ((), ())), preferred_element_type=F32)
        for c in range(n_strip):
            s_ref[h, slot, c, 0:rows, :] = st[:, c * 128:(c + 1) * 128]
        bm_ref[h, slot] = jnp.max(st, axis=0, keepdims=True)

    def sm(h, slot, rows):
        m_old = m_ref[h]
        m_new = jnp.maximum(m_old, bm_ref[h, slot])
        al_ref[h, slot] = jnp.exp2(m_old - m_new)
        m_ref[h] = m_new
        piece = min(ATT_PIECE, rows)
        for c in range(n_strip):
            for r0 in range(0, rows, piece):
                p_ref[h, slot, c, r0:r0 + piece, :] = jnp.exp2(
                    s_ref[h, slot, c, r0:r0 + piece, :] - m_new[:, c * 128:(c + 1) * 128]).astype(BF16)

    def pv(h, vtb, slot, rows):
        p = jnp.concatenate([p_ref[h, slot, c, 0:rows, :] for c in range(n_strip)], axis=1)
        acc_ref[h] = al_ref[h, slot] * acc_ref[h] + jnp.dot(vtb, p, preferred_element_type=F32)

    def k_blk(h, j):
        return k_ref[0, pl.ds(pl.multiple_of(j * tk, tk), tk), hl(h)]

    def vt_blk(h, j):
        return vt_ref[0, hr(h), pl.ds(pl.multiple_of(j * tk, tk), tk)]

    for h in heads:
        qk(h, kc_ref[0, :, hl(h)], 0, nc)
        sm(h, 0, nc)
    if not has_latent:
        for h in heads:
            pv(h, vtc_ref[0, hr(h), :], 0, nc)
    else:
        n_kv = k_ref.shape[1] // tk
        assert n_kv >= 4 and n_kv % 2 == 0
        for h in heads:
            qk(h, k_blk(h, 0), 1, tk)
            pv(h, vtc_ref[0, hr(h), :], 0, nc)
            qk(h, k_blk(h, 1), 0, tk)
            sm(h, 1, tk)

        def body(i, carry):
            for h in heads:
                pv(h, vt_blk(h, 2 * i), 1, tk)
                qk(h, k_blk(h, 2 * i + 2), 1, tk)
                sm(h, 0, tk)
                sm(h, 1, tk)
                pv(h, vt_blk(h, 2 * i + 1), 0, tk)
                qk(h, k_blk(h, 2 * i + 3), 0, tk)
            return carry
        lax.fori_loop(0, (n_kv - 2) // 2, body, 0)
        for h in heads:
            pv(h, vt_blk(h, n_kv - 2), 1, tk)
            sm(h, 0, tk)
            pv(h, vt_blk(h, n_kv - 1), 0, tk)

    dl = dl_ref[...]
    lam = (jnp.exp(jnp.sum(dl[0:1, :] * dl[1:2, :], axis=-1, keepdims=True))
           - jnp.exp(jnp.sum(dl[2:3, :] * dl[3:4, :], axis=-1, keepdims=True)) + lam_init)
    for h in heads:
        num = acc_ref[h, 0:V_DIM, :]
        den = acc_ref[h, V_DIM:V_DIM + 1, :]
        ot = num[:, 0:tq] / den[:, 0:tq] - lam * (num[:, tq:2 * tq] / den[:, tq:2 * tq])
        ot = ot * lax.rsqrt(jnp.mean(ot * ot, axis=0, keepdims=True) + RMS_EPS)
        o_ref[0, :, hl(h)] = (ot.T * sg_ref[...] * (1.0 - lam_init)).astype(BF16)


def diff_attention(q, kc, vtc, k, vt, dlam, sg, lam_init, tq, tk, hpb):
    b, nq, _ = q.shape
    nc = kc.shape[1]
    has_latent = k is not None
    if not has_latent:
        k, vt = kc, vtc
    n = k.shape[1]
    assert nq % tq == 0 and (not has_latent or n % tk == 0) and N_HEADS % hpb == 0
    head_rows = lambda bb, h, i: (bb, i, h)
    head_all = lambda bb, h, i: (bb, 0, h)
    head_all_t = lambda bb, h, i: (bb, h, 0)
    kmax = max(tk, nc)
    return pl.pallas_call(
        functools.partial(_attn_body, lam_init, tk, has_latent),
        grid=(b, N_HEADS // hpb, nq // tq),
        in_specs=[
            pl.BlockSpec((1, tq, hpb * V_DIM), head_rows),
            pl.BlockSpec((1, nc, hpb * V_DIM), head_all),
            pl.BlockSpec((1, hpb * VT_ROWS, nc), head_all_t),
            pl.BlockSpec((1, n, hpb * V_DIM), head_all, pipeline_mode=pl.Buffered(1)),
            pl.BlockSpec((1, hpb * VT_ROWS, n), head_all_t, pipeline_mode=pl.Buffered(1)),
            pl.BlockSpec((4, HEAD_DIM), lambda bb, h, i: (0, 0)),
            pl.BlockSpec((1, V_DIM), lambda bb, h, i: (0, 0)),
        ],
        out_specs=pl.BlockSpec((1, tq, hpb * V_DIM), head_rows),
        out_shape=jax.ShapeDtypeStruct((b, nq, ATT_W), BF16),
        scratch_shapes=[pltpu.VMEM((hpb, 1, 2 * tq), F32), pltpu.VMEM((hpb, VT_ROWS, 2 * tq), F32),
                        pltpu.VMEM((hpb, 2 * tq, V_DIM), BF16),
                        pltpu.VMEM((hpb, 2, 2 * tq // 128, kmax, 128), F32),
                        pltpu.VMEM((hpb, 2, 2 * tq // 128, kmax, 128), BF16),
                        pltpu.VMEM((hpb, 2, 1, 2 * tq), F32), pltpu.VMEM((hpb, 2, 1, 2 * tq), F32)],
        compiler_params=_params(("arbitrary", "arbitrary", "arbitrary")),
        name="diff_attention" if has_latent else "diff_attention_ctx",
    )(q, kc, vtc, k, vt, dlam, sg)


def _s5_body(t_chunk, u_ref, wb_ref, a_ref, wc_ref, dsk_ref, y_ref, xs_ref, st_ref):
    d = pl.program_id(0)
    kk = pl.program_id(1)
    nb = u_ref.shape[1]

    @pl.when(kk == 0)
    def _():
        st_ref[...] = jnp.zeros(st_ref.shape, F32)

    tp = t_chunk // S5_PARTS
    for part in range(S5_PARTS):
        ts = slice(part * tp, (part + 1) * tp)
        u_p = u_ref[ts].reshape(tp * nb, S5_W)
        xs_ref[ts] = jnp.dot(u_p.astype(BF16), wb_ref[0], preferred_element_type=F32).reshape(
            tp, nb, 2 * S5_LANES)
    ar = jnp.broadcast_to(a_ref[0, 0:1, :], (nb, S5_LANES))
    ai = jnp.broadcast_to(a_ref[0, 1:2, :], (nb, S5_LANES))

    def body(i, carry):
        sr, si = carry
        t = jnp.where(d == 0, i, t_chunk - 1 - i)
        xr = xs_ref[t, :, 0:S5_LANES]
        xi = xs_ref[t, :, S5_LANES:2 * S5_LANES]
        nr = ar * sr - ai * si + xr
        ni = ar * si + ai * sr + xi
        xs_ref[t, :, 0:S5_LANES] = nr
        xs_ref[t, :, S5_LANES:2 * S5_LANES] = ni
        return nr, ni

    sr, si = lax.fori_loop(0, t_chunk, body, (st_ref[0], st_ref[1]), unroll=4)
    st_ref[0] = sr
    st_ref[1] = si
    skip = jnp.where(d == 0, 1.0, 0.0) * dsk_ref[...]
    for part in range(S5_PARTS):
        ts = slice(part * tp, (part + 1) * tp)
        y = jnp.dot(xs_ref[ts].reshape(tp * nb, 2 * S5_LANES).astype(BF16), wc_ref[0],
                    preferred_element_type=F32)
        y = y + skip * u_ref[ts].reshape(tp * nb, S5_W)
        y_ref[0, ts] = y.reshape(tp, nb, S5_W)


def s5_scan(u_all, wb, a, wc, dskip, n_ctx, t_chunk):
    tt, nb, _ = u_all.shape
    assert n_ctx % t_chunk == 0 and tt % t_chunk == 0
    nchunk = tt // t_chunk
    cctx = n_ctx // t_chunk

    def chunk_index(d, kk):
        bwd = jnp.where(kk < cctx, cctx - 1 - kk, nchunk - 1 - (kk - cctx))
        return jnp.where(d == 0, kk, bwd)

    return pl.pallas_call(
        functools.partial(_s5_body, t_chunk),
        grid=(2, nchunk),
        in_specs=[
            pl.BlockSpec((t_chunk, nb, S5_W), lambda d, kk: (chunk_index(d, kk), 0, 0)),
            pl.BlockSpec((1, S5_W, 2 * S5_LANES), lambda d, kk: (d, 0, 0)),
            pl.BlockSpec((1, 2, S5_LANES), lambda d, kk: (d, 0, 0)),
            pl.BlockSpec((1, 2 * S5_LANES, S5_W), lambda d, kk: (d, 0, 0)),
            pl.BlockSpec((1, S5_W), lambda d, kk: (0, 0)),
        ],
        out_specs=pl.BlockSpec((1, t_chunk, nb, S5_W), lambda d, kk: (d, chunk_index(d, kk), 0, 0)),
        out_shape=jax.ShapeDtypeStruct((2, tt, nb, S5_W), F32),
        scratch_shapes=[pltpu.VMEM((t_chunk, nb, 2 * S5_LANES), F32), pltpu.VMEM((2, nb, S5_LANES), F32)],
        compiler_params=_params(("arbitrary", "arbitrary")),
        name="s5_scan",
    )(u_all, wb, a, wc, dskip)


def s5_weights(lam_re, lam_im, log_dt, b_re, b_im, c_re, c_im):
    dt = jnp.exp(log_dt.astype(F32))[..., None]
    lr, li = lam_re.astype(F32), lam_im.astype(F32)
    mag = jnp.exp(lr * dt)
    ar, ai = mag * jnp.cos(li * dt), mag * jnp.sin(li * dt)
    den = lr * lr + li * li
    fr = ((ar - 1.0) * lr + ai * li) / den
    fi = (ai * lr - (ar - 1.0) * li) / den
    bbr = fr[..., None] * b_re - fi[..., None] * b_im
    bbi = fr[..., None] * b_im + fi[..., None] * b_re
    eye = jnp.eye(S5_G, dtype=F32)
    def in_map(bb):
        return jnp.einsum('dgnc,gh->dgchn', bb, eye).reshape(2, S5_W, S5_LANES)
    wb = jnp.concatenate([in_map(bbr), in_map(bbi)], axis=-1).astype(BF16)
    def out_map(cc):
        return jnp.einsum('dgcn,gh->dgnhc', cc, eye).reshape(2, S5_LANES, S5_W)
    wc = jnp.concatenate([out_map(c_re.astype(F32)), out_map(-c_im.astype(F32))], axis=1).astype(BF16)
    a = jnp.stack([ar.reshape(2, S5_LANES), ai.reshape(2, S5_LANES)], axis=1)
    return wb, a, wc


def _merge_body(x_ref, oatt_ref, yf_ref, yb_ref, ogm_ref, gate_ref, mod_ref, ng_ref, watt_ref, wglu_ref,
                wgm_ref, wout_ref, wr_ref, xo_ref, h2_ref, pr_ref):
    y_att = jnp.dot(oatt_ref[0], watt_ref[...], preferred_element_type=F32)
    y5 = yf_ref[0] + yb_ref[0]
    z = jnp.dot(jax.nn.gelu(y5).astype(BF16), wglu_ref[...], preferred_element_type=F32)
    y5g = z[:, :D_MODEL] * jax.nn.sigmoid(z[:, D_MODEL:])
    y_gm = jnp.dot(ogm_ref[0], wgm_ref[...], preferred_element_type=F32)
    merged = (gate_ref[0, :, 0:D_MODEL].astype(F32) * y_att
              + gate_ref[0, :, D_MODEL:2 * D_MODEL].astype(F32) * y5g
              + gate_ref[0, :, 2 * D_MODEL:3 * D_MODEL].astype(F32) * y_gm)
    mix = jnp.dot(merged.astype(BF16), wout_ref[...], preferred_element_type=F32)
    mixn = mix * lax.rsqrt(jnp.mean(mix * mix, axis=-1, keepdims=True) + RMS_EPS) * ng_ref[0:1, :]
    xn = x_ref[0] + mod_ref[0, 2:3, :] * mixn
    xo_ref[0] = xn
    h2 = xn * lax.rsqrt(jnp.mean(xn * xn, axis=-1, keepdims=True) + RMS_EPS) * ng_ref[1:2, :]
    h2 = h2 * (1.0 + mod_ref[0, 4:5, :]) + mod_ref[0, 3:4, :]
    h2_ref[0] = h2
    logits = lax.dot_general(wr_ref[...], h2.astype(BF16), (((1,), (1,)), ((), ())),
                             preferred_element_type=F32)
    pe = jnp.exp(logits - jnp.max(logits, axis=0, keepdims=True))
    pr_ref[0] = pe / jnp.sum(pe, axis=0, keepdims=True)


def merge_branches(x, o_att, yf, yb, ogm, gates, modv, ng12, watt, wglu, wgm, wout, wr_t, tm):
    b, n, d = x.shape
    row = lambda bb, j: (bb, j, 0)
    const2 = lambda bb, j: (0, 0)
    return pl.pallas_call(
        _merge_body,
        grid=(b, n // tm),
        in_specs=[
            pl.BlockSpec((1, tm, d), row),
            pl.BlockSpec((1, tm, ATT_W), row),
            pl.BlockSpec((1, tm, S5_W), row),
            pl.BlockSpec((1, tm, S5_W), row),
            pl.BlockSpec((1, tm, GM_W), row),
            pl.BlockSpec((1, tm, 3 * d), row),
            pl.BlockSpec((1, 6, d), lambda bb, j: (bb, 0, 0)),
            pl.BlockSpec((2, d), const2),
            pl.BlockSpec((ATT_W, d), const2),
            pl.BlockSpec((S5_W, 2 * d), const2),
            pl.BlockSpec((GM_W, d), const2),
            pl.BlockSpec((d, d), const2),
            pl.BlockSpec((N_EXP, d), const2),
        ],
        out_specs=[pl.BlockSpec((1, tm, d), row), pl.BlockSpec((1, tm, d), row),
                   pl.BlockSpec((1, N_EXP, tm), lambda bb, j: (bb, 0, j))],
        out_shape=[jax.ShapeDtypeStruct((b, n, d), F32), jax.ShapeDtypeStruct((b, n, d), F32),
                   jax.ShapeDtypeStruct((b, N_EXP, n), F32)],
        compiler_params=_params(("arbitrary", "arbitrary")),
        name="merge_branches",
    )(x, o_att, yf, yb, ogm, gates, modv, ng12, watt, wglu, wgm, wout, wr_t)


def _cumsum_lanes(x, tri):
    outs = []
    run = jnp.zeros((x.shape[0], 1), F32)
    for c in range(x.shape[1] // 128):
        blk = jnp.dot(x[:, c * 128:(c + 1) * 128].astype(BF16), tri, preferred_element_type=F32) + run
        outs.append(blk)
        run = blk[:, 127:128]
    return jnp.concatenate(outs, axis=1)


def _topk_body(cap, p_ref, idx_ref, cs_ref):
    p = p_ref[0]
    ne, n = p.shape
    bits = pltpu.bitcast(p, jnp.int32)
    thr = jnp.zeros((ne, 1), jnp.int32)
    for bit in range(30, -1, -1):
        cand = thr | (1 << bit)
        cnt = jnp.sum(jnp.where(bits >= cand, 1.0, 0.0), axis=-1, keepdims=True)
        thr = jnp.where(cnt >= cap, cand, thr)
    gt = bits > thr
    eq = bits == thr
    n_gt = jnp.sum(jnp.where(gt, 1.0, 0.0), axis=-1, keepdims=True)
    r = lax.broadcasted_iota(jnp.int32, (128, 128), 0)
    c = lax.broadcasted_iota(jnp.int32, (128, 128), 1)
    tri = jnp.where(r <= c, 1.0, 0.0).astype(BF16)
    eq_f = jnp.where(eq, 1.0, 0.0)
    eq_rank = _cumsum_lanes(eq_f, tri) - eq_f
    sel = jnp.where(gt | (eq & (eq_rank < cap - n_gt)), 1.0, 0.0)
    cs_ref[...] = _cumsum_lanes(sel, tri)
    nblk = idx_ref.shape[2] // 128

    def per_expert(e, carry):
        row = cs_ref[pl.ds(e, 1), :]
        blks = []
        for sb in range(nblk):
            slot = (lax.broadcasted_iota(jnp.int32, (128, 1), 0) + sb * 128).astype(F32)
            cnt = jnp.sum(jnp.where(row <= slot, 1.0, 0.0), axis=-1, keepdims=True)
            blks.append(jnp.broadcast_to(cnt, (128, 128)).T[0:1, :])
        idx_ref[0, pl.ds(e, 1), :] = jnp.concatenate(blks, axis=1).astype(jnp.int32)
        return carry
    lax.fori_loop(0, ne, per_expert, 0)


def expert_topk(probs_t, cap):
    b, ne, n = probs_t.shape
    cap_pad = -(-cap // 128) * 128
    return pl.pallas_call(
        functools.partial(_topk_body, cap),
        grid=(b,),
        in_specs=[pl.BlockSpec((1, ne, n), lambda bb: (bb, 0, 0))],
        out_specs=pl.BlockSpec((1, ne, cap_pad), lambda bb: (bb, 0, 0)),
        out_shape=jax.ShapeDtypeStruct((b, ne, cap_pad), jnp.int32),
        scratch_shapes=[pltpu.VMEM((ne, n), F32)],
        compiler_params=_params(("arbitrary",)),
        name="expert_topk",
    )(probs_t)


def _moe_body(cap, nbp, idxa_ref, idxb_ref, idxn_ref, h2_hbm, wr_ref, wg_ref, wu_ref, wd_ref, ye_ref,
              xa_ref, xb_ref, sem):
    e = pl.program_id(0)
    bp = pl.program_id(1)
    step = e * nbp + bp
    last = pl.num_programs(0) * nbp - 1
    bp_next = jnp.minimum(step + 1, last) % nbp

    def row_copy(idx_smem, bb, buf, s, g, r):
        t = idx_smem[0, 0, g * 8 + r]
        return pltpu.make_async_copy(h2_hbm.at[bb, pl.ds(t, 1), :], buf.at[g, pl.ds(r, 1), :], sem.at[s])

    def issue_rolled(idx_smem, bb, buf, s):
        def body(g, carry):
            for r in range(8):
                row_copy(idx_smem, bb, buf, s, g, r).start()
            return carry
        lax.fori_loop(0, cap // 8, body, 0)

    def issue_inline(idx_smem, bb, buf, s):
        for g in range(cap // 8):
            for r in range(8):
                row_copy(idx_smem, bb, buf, s, g, r).start()

    def wait_all(buf, s):
        pltpu.make_async_copy(buf, buf, sem.at[s]).wait()

    def compute(buf, out_i):
        xe = buf[...].reshape(cap, buf.shape[-1]).astype(BF16)
        logits = jnp.dot(xe, wr_ref[...], preferred_element_type=F32)
        pe = jnp.exp(logits - jnp.max(logits, axis=-1, keepdims=True))
        probs = pe / jnp.sum(pe, axis=-1, keepdims=True)
        col = lax.broadcasted_iota(jnp.int32, probs.shape, 1)
        gate = jnp.sum(jnp.where(col == e, probs, 0.0), axis=-1, keepdims=True)
        ye = jnp.zeros((cap, xe.shape[1]), F32)
        for f0 in range(0, wg_ref.shape[2], FF_CHUNK):
            hg = jnp.dot(xe, wg_ref[0, :, f0:f0 + FF_CHUNK], preferred_element_type=F32)
            hu = jnp.dot(xe, wu_ref[0, :, f0:f0 + FF_CHUNK], preferred_element_type=F32)
            hid = (hg * jax.nn.sigmoid(hg) * hu).astype(BF16)
            ye = ye + jnp.dot(hid, wd_ref[0, f0:f0 + FF_CHUNK, :], preferred_element_type=F32)
        ye_ref[out_i, 0] = ye * gate

    @pl.when(step == 0)
    def _():
        issue_rolled(idxa_ref, 2 * bp, xa_ref, 0)

    wait_all(xa_ref, 0)
    issue_inline(idxb_ref, 2 * bp + 1, xb_ref, 1)
    compute(xa_ref, 0)
    wait_all(xb_ref, 1)
    issue_inline(idxn_ref, 2 * bp_next, xa_ref, 0)
    compute(xb_ref, 1)

    @pl.when(step == last)
    def _():
        wait_all(xa_ref, 0)


def moe_experts(h2, idx, cap, wr, wg, wu, wd):
    b, n, d = h2.shape
    ne, cap_pad = idx.shape[1:]
    ff = wg.shape[-1]
    assert b % 2 == 0 and cap % 8 == 0
    nbp = b // 2
    idx3 = idx.reshape(b * ne, 1, cap_pad)

    def idx_a(e, bp):
        return (2 * bp * ne + e, 0, 0)

    def idx_b(e, bp):
        return ((2 * bp + 1) * ne + e, 0, 0)

    def idx_next(e, bp):
        step = jnp.minimum(e * nbp + bp + 1, ne * nbp - 1)
        return (2 * (step % nbp) * ne + step // nbp, 0, 0)

    once = pl.Buffered(1)
    return pl.pallas_call(
        functools.partial(_moe_body, cap, nbp),
        grid=(ne, nbp),
        in_specs=[
            pl.BlockSpec((1, 1, cap_pad), idx_a, memory_space=pltpu.SMEM),
            pl.BlockSpec((1, 1, cap_pad), idx_b, memory_space=pltpu.SMEM),
            pl.BlockSpec((1, 1, cap_pad), idx_next, memory_space=pltpu.SMEM),
            pl.BlockSpec(memory_space=pl.ANY),
            pl.BlockSpec((d, ne), lambda e, bp: (0, 0)),
            pl.BlockSpec((1, d, ff), lambda e, bp: (e, 0, 0), pipeline_mode=once),
            pl.BlockSpec((1, d, ff), lambda e, bp: (e, 0, 0), pipeline_mode=once),
            pl.BlockSpec((1, ff, d), lambda e, bp: (e, 0, 0), pipeline_mode=once),
        ],
        out_specs=pl.BlockSpec((2, 1, cap, d), lambda e, bp: (bp, e, 0, 0)),
        out_shape=jax.ShapeDtypeStruct((b, ne, cap, d), F32),
        scratch_shapes=[pltpu.VMEM((cap // 8, 8, d), F32), pltpu.VMEM((cap // 8, 8, d), F32),
                        pltpu.SemaphoreType.DMA((2,))],
        compiler_params=_params(("arbitrary", "arbitrary")),
        name="moe_experts",
    )(idx3, idx3, idx3, h2, wr, wg, wu, wd)


def _combine_body(cap, idx_ref, ye_ref, acc_ref):
    e = pl.program_id(1)

    @pl.when(e == 0)
    def _():
        acc_ref[...] = jnp.zeros(acc_ref.shape, F32)

    def body(g, carry):
        ts = [idx_ref[0, 0, g * COMBINE_GROUP + r] for r in range(COMBINE_GROUP)]
        rows = [acc_ref[0, pl.ds(ts[r], 1), :] + ye_ref[0, 0, pl.ds(g * COMBINE_GROUP + r, 1), :]
                for r in range(COMBINE_GROUP)]
        for r in range(COMBINE_GROUP):
            acc_ref[0, pl.ds(ts[r], 1), :] = rows[r]
        return carry
    lax.fori_loop(0, cap // COMBINE_GROUP, body, 0)


def moe_combine(ye, idx, n):
    b, ne, cap, d = ye.shape
    cap_pad = idx.shape[2]
    idx3 = idx.reshape(b * ne, 1, cap_pad)
    assert n * d * 4 <= V7X_VMEM_BYTES // 2
    return pl.pallas_call(
        functools.partial(_combine_body, cap),
        grid=(b, ne),
        in_specs=[
            pl.BlockSpec((1, 1, cap_pad), lambda bb, e: (bb * ne + e, 0, 0), memory_space=pltpu.SMEM),
            pl.BlockSpec((1, 1, cap, d), lambda bb, e: (bb, e, 0, 0)),
        ],
        out_specs=pl.BlockSpec((1, n, d), lambda bb, e: (bb, 0, 0), pipeline_mode=pl.Buffered(1)),
        out_shape=jax.ShapeDtypeStruct((b, n, d), F32),
        compiler_params=_params(("arbitrary", "arbitrary")),
        name="moe_combine",
    )(idx3, ye)


def _resid_body(x_ref, y_ref, mod_ref, ng_ref, o_ref):
    y = y_ref[0]
    yn = y * lax.rsqrt(jnp.mean(y * y, axis=-1, keepdims=True) + RMS_EPS) * ng_ref[...]
    o_ref[0] = x_ref[0] + mod_ref[0, 5:6, :] * yn


def moe_residual(x, y, modv, ng3, tm):
    b, n, d = x.shape
    row = lambda bb, j: (bb, j, 0)
    return pl.pallas_call(
        _resid_body,
        grid=(b, n // tm),
        in_specs=[pl.BlockSpec((1, tm, d), row), pl.BlockSpec((1, tm, d), row),
                  pl.BlockSpec((1, 6, d), lambda bb, j: (bb, 0, 0)), pl.BlockSpec((1, d), lambda bb, j: (0, 0))],
        out_specs=pl.BlockSpec((1, tm, d), row),
        out_shape=jax.ShapeDtypeStruct((b, n, d), F32),
        compiler_params=_params(("arbitrary", "arbitrary")),
        name="moe_residual",
    )(x, y, modv, ng3)


def _rope_tables(n):
    rows = n // GRID_W
    row = jnp.repeat(jnp.arange(rows), GRID_W).astype(F32)
    col = jnp.tile(jnp.arange(GRID_W), rows).astype(F32)
    inv = ROPE_THETA ** (-jnp.arange(ROPE_FREQS, dtype=F32) / ROPE_FREQS)
    ar, ac = row[:, None] * inv, col[:, None] * inv
    cos64 = jnp.concatenate([jnp.cos(ar), jnp.cos(ar), jnp.cos(ac), jnp.cos(ac)], axis=1)
    sin64 = jnp.concatenate([-jnp.sin(ar), jnp.sin(ar), -jnp.sin(ac), jnp.sin(ac)], axis=1)
    return jnp.tile(cos64, (1, 2)), jnp.tile(sin64, (1, 2))


def _moe_sublayer(x_mid, h2, probs_t, modv, ng3, wr, wg, wu, wd, tm):
    n = x_mid.shape[1]
    cap = CAP_FACTOR * n // N_EXP
    idx = expert_topk(probs_t, cap)
    ye = moe_experts(h2, idx, cap, wr, wg, wu, wd)
    y = moe_combine(ye, idx, n)
    return moe_residual(x_mid, y, modv, ng3, tm)


def _forward(x, c, ctx, c_ctx, w_mod, b_mod, norm_g, w_in, diff_lambda, diff_subln_g, s5_lam_re, s5_lam_im,
             s5_log_dt, s5_b_re, s5_b_im, s5_c_re, s5_c_im, s5_d, w_s5_glu, gmlp_norm_g, gmlp_ws, gmlp_bs,
             w_br_att, w_br_gmlp, w_out, w_router, w_e_gate, w_e_up, w_e_down, *, tm, tq, tk, t_chunk, hpb):
    b, n, d = x.shape
    n_ctx = ctx.shape[1]
    depth = w_mod.shape[0]
    tm_c = min(tm, n_ctx)
    tq_c = min(tq, n_ctx)

    c_rows = jnp.zeros((MOD_ROWS, d), F32).at[:b].set(c).at[b].set(c_ctx)
    mod_all = modulation(c_rows, w_mod, b_mod).reshape(depth, MOD_ROWS, 6, d)
    cos_t, sin_t = _rope_tables(n)
    cos_c, sin_c = jnp.ones((n_ctx, 128), F32), jnp.zeros((n_ctx, 128), F32)

    xc = ctx
    for l in range(depth):
        last = l == depth - 1
        lam_init = 0.8 - 0.6 * math.exp(-0.3 * l)
        modv = mod_all[l, :b]
        modc = jnp.broadcast_to(mod_all[l, b:b + 1], (b, 6, d))
        ng0 = norm_g[l, 0:1]
        w_in_b = w_in[l].astype(BF16)
        gg = gmlp_norm_g[l][None, :]
        ws = gmlp_ws[l].astype(BF16)
        bs = jnp.repeat(gmlp_bs[l].T, GM_HD, axis=1)
        sg = diff_subln_g[l][None, :]
        wr = w_router[l].astype(BF16)
        wg, wu, wd = w_e_gate[l].astype(BF16), w_e_up[l].astype(BF16), w_e_down[l].astype(BF16)

        q, k, v, u, ogm, gates = input_projection(x, modv, ng0, w_in_b, cos_t, sin_t, gg, ws, bs, True, tm)
        pc = input_projection(xc, modc, ng0, w_in_b, cos_c, sin_c, gg, ws, bs, not last, tm_c)
        if last:
            kc, vc, uc = pc
        else:
            qc, kc, vc, uc, ogm_c, gates_c = pc

        o_att = diff_attention(q, kc, vc, k, v, diff_lambda[l], sg, lam_init, tq, tk, hpb)

        wb, a, wc = s5_weights(s5_lam_re[l], s5_lam_im[l], s5_log_dt[l], s5_b_re[l], s5_b_im[l],
                               s5_c_re[l], s5_c_im[l])
        u_all = jnp.concatenate([uc, u], axis=1).transpose(1, 0, 2)
        y5 = s5_scan(u_all, wb, a, wc, s5_d[l][None, :], n_ctx, min(t_chunk, n_ctx)).transpose(0, 2, 1, 3)

        merge_w = (norm_g[l, 1:3], w_br_att[l].astype(BF16), w_s5_glu[l].astype(BF16),
                   w_br_gmlp[l].astype(BF16), w_out[l].astype(BF16), w_router[l].T.astype(BF16))
        x_mid, h2, probs_t = merge_branches(x, o_att, y5[0, :, n_ctx:], y5[1, :, n_ctx:], ogm, gates, modv,
                                            *merge_w, tm)
        x = _moe_sublayer(x_mid, h2, probs_t, modv, norm_g[l, 3:4], wr, wg, wu, wd, tm)
        if not last:
            o_att_c = diff_attention(qc, kc, vc, None, None, diff_lambda[l], sg, lam_init, tq_c, tk, hpb)
            xc_mid, h2c, probs_c = merge_branches(xc, o_att_c, y5[0, :, :n_ctx], y5[1, :, :n_ctx], ogm_c,
                                                  gates_c, modc, *merge_w, tm_c)
            xc = _moe_sublayer(xc_mid, h2c, probs_c, modc, norm_g[l, 3:4], wr, wg, wu, wd, tm_c)
    return x


def kernel(x, c, ctx, c_ctx, w_mod, b_mod, norm_g, w_in, diff_lambda, diff_subln_g, s5_lam_re, s5_lam_im, s5_log_dt, s5_b_re, s5_b_im, s5_c_re, s5_c_im, s5_d, w_s5_glu, gmlp_norm_g, gmlp_ws, gmlp_bs, w_br_att, w_br_gmlp, w_out, w_router, w_e_gate, w_e_up, w_e_down):
    return _forward(x, c, ctx, c_ctx, w_mod, b_mod, norm_g, w_in, diff_lambda, diff_subln_g, s5_lam_re, s5_lam_im,
                    s5_log_dt, s5_b_re, s5_b_im, s5_c_re, s5_c_im, s5_d, w_s5_glu, gmlp_norm_g, gmlp_ws, gmlp_bs,
                    w_br_att, w_br_gmlp, w_out, w_router, w_e_gate, w_e_up, w_e_down,
                    tm=512, tq=512, tk=512, t_chunk=256, hpb=4)
```

```python
import functools
import math

import jax
import jax.numpy as jnp
from jax import lax
from jax.experimental import pallas as pl
from jax.experimental.pallas import tpu as pltpu

F32 = jnp.float32
BF16 = jnp.bfloat16

D_MODEL = 1024
GRID_W = 64
RMS_EPS = 1e-6
N_HEADS = 4
HEAD_DIM = 64
V_DIM = 2 * HEAD_DIM
ATT_W = N_HEADS * V_DIM
ROPE_THETA = 10000.0
ROPE_FREQS = HEAD_DIM // 4
S5_W = D_MODEL // 4
S5_P = 16
S5_G = S5_W // S5_P
S5_N = 64
S5_LANES = S5_G * S5_N
GM_W = D_MODEL // 4
GM_H = 4
GM_HD = GM_W // GM_H
CHUNK = 128
N_EXP = 16
CAP_FACTOR = 2
K_END = ATT_W
V_END = K_END + ATT_W
S5_END = V_END + S5_W
Q_END = S5_END + ATT_W
GMLP_END = Q_END + 2 * GM_W
IN_COLS = GMLP_END + 3 * D_MODEL
LOG2E = 1.4426950408889634

V7X_VMEM_BYTES = 64 * 1024 * 1024
VMEM_LIMIT = 56 * 1024 * 1024
FF_CHUNK = 512
S5_PARTS = 4
COMBINE_GROUP = 4
ATT_PIECE = 256
VT_ROWS = V_DIM + 16
MOD_ROWS = 16


def _params(sem):
    return pltpu.CompilerParams(dimension_semantics=sem, vmem_limit_bytes=VMEM_LIMIT)


def _mod_body(c_ref, w_ref, b_ref, o_ref):
    c = c_ref[...]
    s = (c * jax.nn.sigmoid(c)).astype(BF16)
    o_ref[0] = jnp.dot(s, w_ref[0].astype(BF16), preferred_element_type=F32) + b_ref[0]


def modulation(c_rows, w_mod, b_mod, tn=1536):
    depth, d, n6 = w_mod.shape
    return pl.pallas_call(
        _mod_body,
        grid=(depth, n6 // tn),
        in_specs=[
            pl.BlockSpec((MOD_ROWS, d), lambda l, j: (0, 0)),
            pl.BlockSpec((1, d, tn), lambda l, j: (l, 0, j)),
            pl.BlockSpec((1, 1, tn), lambda l, j: (l, 0, j)),
        ],
        out_specs=pl.BlockSpec((1, MOD_ROWS, tn), lambda l, j: (l, 0, j)),
        out_shape=jax.ShapeDtypeStruct((depth, MOD_ROWS, n6), F32),
        compiler_params=_params(("arbitrary", "arbitrary")),
        name="modulation",
    )(c_rows, w_mod, b_mod.reshape(depth, 1, n6))


def _rope_cols(p, cos, sin):
    lane = lax.broadcasted_iota(jnp.int32, (p.shape[0], 128), 1)
    first = (lane % 32) < ROPE_FREQS
    outs = []
    for s in range(p.shape[1] // 128):
        ps = p[:, s * 128:(s + 1) * 128]
        sw = jnp.where(first, pltpu.roll(ps, 128 - ROPE_FREQS, 1), pltpu.roll(ps, ROPE_FREQS, 1))
        outs.append(ps * cos + sw * sin)
    return jnp.concatenate(outs, axis=1)


def _inproj_body(full, x_ref, mod_ref, ng_ref, w_ref, cos_ref, sin_ref, gg_ref, ws_ref, bs_ref, *outs):
    xt = x_ref[0]
    h = xt * lax.rsqrt(jnp.mean(xt * xt, axis=-1, keepdims=True) + RMS_EPS) * ng_ref[...]
    h = h * (1.0 + mod_ref[0, 1:2, :]) + mod_ref[0, 0:1, :]
    hb = h.astype(BF16)

    def proj(a, b):
        return jnp.dot(hb, w_ref[:, a:b], preferred_element_type=F32)

    cos = cos_ref[...]
    sin = sin_ref[...]
    if full:
        q_ref, k_ref, v_ref, u_ref, ogm_ref, gate_ref = outs
    else:
        k_ref, v_ref, u_ref = outs
    k_ref[0] = _rope_cols(proj(0, K_END), cos, sin).astype(BF16)
    vt = proj(K_END, V_END).T
    pad_row = lax.broadcasted_iota(jnp.int32, (VT_ROWS - V_DIM, vt.shape[1]), 0)
    ones_pad = jnp.where(pad_row == 0, 1.0, 0.0).astype(BF16)
    for hh in range(N_HEADS):
        v_ref[0, hh * VT_ROWS:hh * VT_ROWS + V_DIM, :] = vt[hh * V_DIM:(hh + 1) * V_DIM, :].astype(BF16)
        v_ref[0, hh * VT_ROWS + V_DIM:(hh + 1) * VT_ROWS, :] = ones_pad
    nb = u_ref.shape[1] // xt.shape[0]
    pu = proj(V_END, S5_END)
    for s in range(S5_W // 128):
        u_ref[s, pl.ds(pl.program_id(1), xt.shape[0], stride=nb), :] = pu[:, s * 128:(s + 1) * 128]
    if not full:
        return
    q_ref[0] = (_rope_cols(proj(S5_END, Q_END), cos, sin) * (HEAD_DIM ** -0.5 * LOG2E)).astype(BF16)

    zg = jax.nn.gelu(proj(Q_END, GMLP_END))
    zu = zg[:, :GM_W]
    zv = zg[:, GM_W:]
    mu = jnp.mean(zv, axis=-1, keepdims=True)
    zc = zv - mu
    zvn = (zc * lax.rsqrt(jnp.mean(zc * zc, axis=-1, keepdims=True) + RMS_EPS) * gg_ref[...]).astype(BF16)
    head = lax.broadcasted_iota(jnp.int32, (CHUNK, GM_W), 1) // GM_HD
    tm = xt.shape[0]
    for c in range(tm // CHUNK):
        zc_blk = zvn[c * CHUNK:(c + 1) * CHUNK, :]
        mixed = bs_ref[...]
        for g in range(GM_H):
            mixed = mixed + jnp.dot(ws_ref[g], jnp.where(head == g, zc_blk, jnp.zeros_like(zc_blk)),
                                    preferred_element_type=F32)
        ogm_ref[0, c * CHUNK:(c + 1) * CHUNK, :] = (zu[c * CHUNK:(c + 1) * CHUNK, :] * mixed).astype(BF16)

    for j in range(3):
        a = GMLP_END + j * D_MODEL
        gate_ref[0, :, j * D_MODEL:(j + 1) * D_MODEL] = jax.nn.sigmoid(proj(a, a + D_MODEL)).astype(BF16)


def input_projection(x, modv, ng, w_in, cos_t, sin_t, gg, ws, bs, full, tm):
    b, n, d = x.shape
    assert n % tm == 0 and tm % CHUNK == 0
    row = lambda j, bb: (bb, j, 0)
    const2 = lambda j, bb: (0, 0)
    out_shape = [jax.ShapeDtypeStruct((b, n, ATT_W), BF16),
                 jax.ShapeDtypeStruct((b, N_HEADS * VT_ROWS, n), BF16),
                 jax.ShapeDtypeStruct((S5_W // 128, n * b, 128), F32)]
    out_specs = [pl.BlockSpec((1, tm, ATT_W), row),
                 pl.BlockSpec((1, N_HEADS * VT_ROWS, tm), lambda j, bb: (bb, 0, j)),
                 pl.BlockSpec((S5_W // 128, tm * b, 128), lambda j, bb: (0, j, 0))]
    if full:
        out_shape = [jax.ShapeDtypeStruct((b, n, ATT_W), BF16)] + out_shape + [
            jax.ShapeDtypeStruct((b, n, GM_W), BF16), jax.ShapeDtypeStruct((b, n, 3 * d), BF16)]
        out_specs = [pl.BlockSpec((1, tm, ATT_W), row)] + out_specs + [
            pl.BlockSpec((1, tm, GM_W), row), pl.BlockSpec((1, tm, 3 * d), row)]
    in_specs = [
        pl.BlockSpec((1, tm, d), row),
        pl.BlockSpec((1, 6, d), lambda j, bb: (bb, 0, 0)),
        pl.BlockSpec((1, d), const2),
        pl.BlockSpec((d, IN_COLS), const2),
        pl.BlockSpec((tm, 128), lambda j, bb: (j, 0)),
        pl.BlockSpec((tm, 128), lambda j, bb: (j, 0)),
        pl.BlockSpec((1, GM_W), const2),
        pl.BlockSpec((GM_H, CHUNK, CHUNK), lambda j, bb: (0, 0, 0)),
        pl.BlockSpec((CHUNK, GM_W), const2),
    ]
    return pl.pallas_call(
        functools.partial(_inproj_body, full),
        grid=(n // tm, b),
        in_specs=in_specs,
        out_specs=out_specs,
        out_shape=out_shape,
        compiler_params=_params(("arbitrary", "arbitrary")),
        name="input_projection" if full else "input_projection_ctx_side",
    )(x, modv, ng, w_in, cos_t, sin_t, gg, ws, bs)


def _attn_body(lam_init, tk, has_latent, q_ref, kc_ref, vtc_ref, k_ref, vt_ref, dl_ref, sg_ref, o_ref,
               m_ref, acc_ref, q2_ref, s_ref, p_ref, al_ref, bm_ref):
    tq = q_ref.shape[1]
    nc = kc_ref.shape[1]
    n_strip = 2 * tq // 128
    heads = range(q_ref.shape[2] // V_DIM)

    def hl(h):
        return slice(h * V_DIM, (h + 1) * V_DIM)

    def hr(h):
        return slice(h * VT_ROWS, (h + 1) * VT_ROWS)

    for h in heads:
        qt = q_ref[0, :, hl(h)].astype(F32).T
        dim = lax.broadcasted_iota(jnp.int32, qt.shape, 0)
        q2_ref[h, :, 0:tq] = jnp.where(dim < HEAD_DIM, qt, 0.0).astype(BF16)
        q2_ref[h, :, tq:2 * tq] = jnp.where(dim >= HEAD_DIM, qt, 0.0).astype(BF16)
    m_ref[...] = jnp.full(m_ref.shape, -jnp.inf, F32)
    acc_ref[...] = jnp.zeros(acc_ref.shape, F32)

    def qk(h, kb, slot, rows):
        st = jnp.dot(kb, q2_ref[h], preferred_element_type=F32)
        for c in range(n_strip):
            s_ref[h, slot, c, 0:rows, :] = st[:, c * 128:(c + 1) * 128]
        bm_ref[h, slot] = jnp.max(st, axis=0, keepdims=True)

    def sm(h, slot, rows):
        m_old = m_ref[h]
        m_new = jnp.maximum(m_old, bm_ref[h, slot])
        al_ref[h, slot] = jnp.exp2(m_old - m_new)
        m_ref[h] = m_new
        piece = min(ATT_PIECE, rows)
        for c in range(n_strip):
            for r0 in range(0, rows, piece):
                p_ref[h, slot, c, r0:r0 + piece, :] = jnp.exp2(
                    s_ref[h, slot, c, r0:r0 + piece, :] - m_new[:, c * 128:(c + 1) * 128]).astype(BF16)

    def pv(h, vtb, slot, rows):
        p = jnp.concatenate([p_ref[h, slot, c, 0:rows, :] for c in range(n_strip)], axis=1)
        acc_ref[h] = al_ref[h, slot] * acc_ref[h] + jnp.dot(vtb, p, preferred_element_type=F32)

    def k_blk(h, j):
        return k_ref[0, pl.ds(pl.multiple_of(j * tk, tk), tk), hl(h)]

    def vt_blk(h, j):
        return vt_ref[0, hr(h), pl.ds(pl.multiple_of(j * tk, tk), tk)]

    for h in heads:
        qk(h, kc_ref[0, :, hl(h)], 0, nc)
        sm(h, 0, nc)
    if not has_latent:
        for h in heads:
            pv(h, vtc_ref[0, hr(h), :], 0, nc)
    else:
        n_kv = k_ref.shape[1] // tk
        assert n_kv >= 4 and n_kv % 2 == 0
        for h in heads:
            qk(h, k_blk(h, 0), 1, tk)
            pv(h, vtc_ref[0, hr(h), :], 0, nc)
            qk(h, k_blk(h, 1), 0, tk)
            sm(h, 1, tk)

        def body(i, carry):
            for h in heads:
                pv(h, vt_blk(h, 2 * i), 1, tk)
                qk(h, k_blk(h, 2 * i + 2), 1, tk)
                sm(h, 0, tk)
                sm(h, 1, tk)
                pv(h, vt_blk(h, 2 * i + 1), 0, tk)
                qk(h, k_blk(h, 2 * i + 3), 0, tk)
            return carry
        lax.fori_loop(0, (n_kv - 2) // 2, body, 0)
        for h in heads:
            pv(h, vt_blk(h, n_kv - 2), 1, tk)
            sm(h, 0, tk)
            pv(h, vt_blk(h, n_kv - 1), 0, tk)

    dl = dl_ref[...]
    lam = (jnp.exp(jnp.sum(dl[0:1, :] * dl[1:2, :], axis=-1, keepdims=True))
           - jnp.exp(jnp.sum(dl[2:3, :] * dl[3:4, :], axis=-1, keepdims=True)) + lam_init)
    for h in heads:
        num = acc_ref[h, 0:V_DIM, :]
        den = acc_ref[h, V_DIM:V_DIM + 1, :]
        ot = num[:, 0:tq] / den[:, 0:tq] - lam * (num[:, tq:2 * tq] / den[:, tq:2 * tq])
        ot = ot * lax.rsqrt(jnp.mean(ot * ot, axis=0, keepdims=True) + RMS_EPS)
        o_ref[0, :, hl(h)] = (ot.T * sg_ref[...] * (1.0 - lam_init)).astype(BF16)


def diff_attention(q, kc, vtc, k, vt, dlam, sg, lam_init, tq, tk, hpb):
    b, nq, _ = q.shape
    nc = kc.shape[1]
    has_latent = k is not None
    if not has_latent:
        k, vt = kc, vtc
    n = k.shape[1]
    assert nq % tq == 0 and (not has_latent or n % tk == 0) and N_HEADS % hpb == 0
    head_rows = lambda bb, h, i: (bb, i, h)
    head_all = lambda bb, h, i: (bb, 0, h)
    head_all_t = lambda bb, h, i: (bb, h, 0)
    kmax = max(tk, nc)
    return pl.pallas_call(
        functools.partial(_attn_body, lam_init, tk, has_latent),
        grid=(b, N_HEADS // hpb, nq // tq),
        in_specs=[
            pl.BlockSpec((1, tq, hpb * V_DIM), head_rows),
            pl.BlockSpec((1, nc, hpb * V_DIM), head_all),
            pl.BlockSpec((1, hpb * VT_ROWS, nc), head_all_t),
            pl.BlockSpec((1, n, hpb * V_DIM), head_all, pipeline_mode=pl.Buffered(1)),
            pl.BlockSpec((1, hpb * VT_ROWS, n), head_all_t, pipeline_mode=pl.Buffered(1)),
            pl.BlockSpec((4, HEAD_DIM), lambda bb, h, i: (0, 0)),
            pl.BlockSpec((1, V_DIM), lambda bb, h, i: (0, 0)),
        ],
        out_specs=pl.BlockSpec((1, tq, hpb * V_DIM), head_rows),
        out_shape=jax.ShapeDtypeStruct((b, nq, ATT_W), BF16),
        scratch_shapes=[pltpu.VMEM((hpb, 1, 2 * tq), F32), pltpu.VMEM((hpb, VT_ROWS, 2 * tq), F32),
                        pltpu.VMEM((hpb, V_DIM, 2 * tq), BF16),
                        pltpu.VMEM((hpb, 2, 2 * tq // 128, kmax, 128), F32),
                        pltpu.VMEM((hpb, 2, 2 * tq // 128, kmax, 128), BF16),
                        pltpu.VMEM((hpb, 2, 1, 2 * tq), F32), pltpu.VMEM((hpb, 2, 1, 2 * tq), F32)],
        compiler_params=_params(("arbitrary", "arbitrary", "arbitrary")),
        name="diff_attention" if has_latent else "diff_attention_ctx",
    )(q, kc, vtc, k, vt, dlam, sg)


def _s5_body(t_chunk, nb, cctx, ul_ref, uc_ref, wb_ref, a_ref, wc_ref, dsk_ref, y_ref, xs_ref, st_ref):
    d = pl.program_id(0)
    kk = pl.program_id(1)
    n_slab = ul_ref.shape[0]

    @pl.when(kk == 0)
    def _():
        st_ref[...] = jnp.zeros(st_ref.shape, F32)

    tp = t_chunk // S5_PARTS
    from_ctx = kk < cctx

    def u_rows(part):
        rs = slice(part * tp * nb, (part + 1) * tp * nb)
        return jnp.concatenate([jnp.where(from_ctx, uc_ref[s, rs, :], ul_ref[s, rs, :]) for s in range(n_slab)],
                               axis=1)

    for part in range(S5_PARTS):
        xs_ref[part * tp:(part + 1) * tp] = jnp.dot(u_rows(part).astype(BF16), wb_ref[0],
                                                    preferred_element_type=F32).reshape(tp, nb, 2 * S5_LANES)
    ar = jnp.broadcast_to(a_ref[0, 0:1, :], (nb, S5_LANES))
    ai = jnp.broadcast_to(a_ref[0, 1:2, :], (nb, S5_LANES))

    def body(i, carry):
        sr, si = carry
        t = jnp.where(d == 0, i, t_chunk - 1 - i)
        xr = xs_ref[t, :, 0:S5_LANES]
        xi = xs_ref[t, :, S5_LANES:2 * S5_LANES]
        nr = ar * sr - ai * si + xr
        ni = ar * si + ai * sr + xi
        xs_ref[t, :, 0:S5_LANES] = nr
        xs_ref[t, :, S5_LANES:2 * S5_LANES] = ni
        return nr, ni

    sr, si = lax.fori_loop(0, t_chunk, body, (st_ref[0], st_ref[1]), unroll=4)
    st_ref[0] = sr
    st_ref[1] = si
    skip = jnp.where(d == 0, 1.0, 0.0) * dsk_ref[...]
    for part in range(S5_PARTS):
        y = jnp.dot(xs_ref[part * tp:(part + 1) * tp].reshape(tp * nb, 2 * S5_LANES).astype(BF16), wc_ref[0],
                    preferred_element_type=F32)
        y = y + skip * u_rows(part)
        for s in range(n_slab):
            y_ref[0, s, part * tp * nb:(part + 1) * tp * nb, :] = y[:, s * 128:(s + 1) * 128]


def s5_scan(u_lat, u_ctx, nb, wb, a, wc, dskip, t_chunk):
    n_slab, rows_lat, _ = u_lat.shape
    rows = rows_lat + u_ctx.shape[1]
    assert rows_lat % (t_chunk * nb) == 0 and u_ctx.shape[1] % (t_chunk * nb) == 0
    nchunk = rows // (t_chunk * nb)
    nlat = rows_lat // (t_chunk * nb)
    cctx = nchunk - nlat

    def chunk_index(d, kk):
        fwd = jnp.where(kk < cctx, nlat + kk, kk - cctx)
        return jnp.where(d == 0, fwd, nchunk - 1 - kk)

    def lat_index(d, kk):
        return jnp.clip(chunk_index(d, kk), 0, nlat - 1)

    def ctx_index(d, kk):
        return jnp.clip(chunk_index(d, kk) - nlat, 0, cctx - 1)

    return pl.pallas_call(
        functools.partial(_s5_body, t_chunk, nb, cctx),
        grid=(2, nchunk),
        in_specs=[
            pl.BlockSpec((n_slab, t_chunk * nb, 128), lambda d, kk: (0, lat_index(d, kk), 0)),
            pl.BlockSpec((n_slab, t_chunk * nb, 128), lambda d, kk: (0, ctx_index(d, kk), 0)),
            pl.BlockSpec((1, S5_W, 2 * S5_LANES), lambda d, kk: (d, 0, 0)),
            pl.BlockSpec((1, 2, S5_LANES), lambda d, kk: (d, 0, 0)),
            pl.BlockSpec((1, 2 * S5_LANES, S5_W), lambda d, kk: (d, 0, 0)),
            pl.BlockSpec((1, S5_W), lambda d, kk: (0, 0)),
        ],
        out_specs=pl.BlockSpec((1, n_slab, t_chunk * nb, 128), lambda d, kk: (d, 0, chunk_index(d, kk), 0)),
        out_shape=jax.ShapeDtypeStruct((2, n_slab, rows, 128), F32),
        scratch_shapes=[pltpu.VMEM((t_chunk, nb, 2 * S5_LANES), F32), pltpu.VMEM((2, nb, S5_LANES), F32)],
        compiler_params=_params(("arbitrary", "arbitrary")),
        name="s5_scan",
    )(u_lat, u_ctx, wb, a, wc, dskip)


def s5_weights(lam_re, lam_im, log_dt, b_re, b_im, c_re, c_im):
    dt = jnp.exp(log_dt.astype(F32))[..., None]
    lr, li = lam_re.astype(F32), lam_im.astype(F32)
    mag = jnp.exp(lr * dt)
    ar, ai = mag * jnp.cos(li * dt), mag * jnp.sin(li * dt)
    den = lr * lr + li * li
    fr = ((ar - 1.0) * lr + ai * li) / den
    fi = (ai * lr - (ar - 1.0) * li) / den
    bbr = fr[..., None] * b_re - fi[..., None] * b_im
    bbi = fr[..., None] * b_im + fi[..., None] * b_re
    eye = jnp.eye(S5_G, dtype=F32)
    def in_map(bb):
        return jnp.einsum('dgnc,gh->dgchn', bb, eye).reshape(2, S5_W, S5_LANES)
    wb = jnp.concatenate([in_map(bbr), in_map(bbi)], axis=-1).astype(BF16)
    def out_map(cc):
        return jnp.einsum('dgcn,gh->dgnhc', cc, eye).reshape(2, S5_LANES, S5_W)
    wc = jnp.concatenate([out_map(c_re.astype(F32)), out_map(-c_im.astype(F32))], axis=1).astype(BF16)
    a = jnp.stack([ar.reshape(2, S5_LANES), ai.reshape(2, S5_LANES)], axis=1)
    return wb, a, wc


def _merge_body(x_ref, oatt_ref, yf_ref, yb_ref, ogm_ref, gate_ref, mod_ref, ng_ref, watt_ref, wglu_ref,
                wgm_ref, wout_ref, wr_ref, xo_ref, h2_ref, pr_ref):
    y_att = jnp.dot(oatt_ref[0], watt_ref[...], preferred_element_type=F32)
    tm = x_ref.shape[1]
    nb = yf_ref.shape[2] // tm
    rows = pl.ds(pl.program_id(1), tm, stride=nb)
    y5 = jnp.concatenate([yf_ref[0, s, rows, :] + yb_ref[0, s, rows, :] for s in range(yf_ref.shape[1])], axis=1)
    z = jnp.dot(jax.nn.gelu(y5).astype(BF16), wglu_ref[...], preferred_element_type=F32)
    y5g = z[:, :D_MODEL] * jax.nn.sigmoid(z[:, D_MODEL:])
    y_gm = jnp.dot(ogm_ref[0], wgm_ref[...], preferred_element_type=F32)
    merged = (gate_ref[0, :, 0:D_MODEL].astype(F32) * y_att
              + gate_ref[0, :, D_MODEL:2 * D_MODEL].astype(F32) * y5g
              + gate_ref[0, :, 2 * D_MODEL:3 * D_MODEL].astype(F32) * y_gm)
    mix = jnp.dot(merged.astype(BF16), wout_ref[...], preferred_element_type=F32)
    mixn = mix * lax.rsqrt(jnp.mean(mix * mix, axis=-1, keepdims=True) + RMS_EPS) * ng_ref[0:1, :]
    xn = x_ref[0] + mod_ref[0, 2:3, :] * mixn
    xo_ref[0] = xn
    h2 = xn * lax.rsqrt(jnp.mean(xn * xn, axis=-1, keepdims=True) + RMS_EPS) * ng_ref[1:2, :]
    h2 = h2 * (1.0 + mod_ref[0, 4:5, :]) + mod_ref[0, 3:4, :]
    h2_ref[0] = h2
    logits = lax.dot_general(wr_ref[...], h2.astype(BF16), (((1,), (1,)), ((), ())),
                             preferred_element_type=F32)
    pe = jnp.exp(logits - jnp.max(logits, axis=0, keepdims=True))
    pr_ref[0] = pe / jnp.sum(pe, axis=0, keepdims=True)


def merge_branches(x, o_att, y5, t_offset, ogm, gates, modv, ng12, watt, wglu, wgm, wout, wr_t, tm):
    b, n, d = x.shape
    n_slab = y5.shape[1]
    assert t_offset % tm == 0
    row = lambda j, bb: (bb, j, 0)
    const2 = lambda j, bb: (0, 0)
    return pl.pallas_call(
        _merge_body,
        grid=(n // tm, b),
        in_specs=[
            pl.BlockSpec((1, tm, d), row),
            pl.BlockSpec((1, tm, ATT_W), row),
            pl.BlockSpec((1, n_slab, tm * b, 128), lambda j, bb: (0, 0, t_offset // tm + j, 0)),
            pl.BlockSpec((1, n_slab, tm * b, 128), lambda j, bb: (1, 0, t_offset // tm + j, 0)),
            pl.BlockSpec((1, tm, GM_W), row),
            pl.BlockSpec((1, tm, 3 * d), row),
            pl.BlockSpec((1, 6, d), lambda j, bb: (bb, 0, 0)),
            pl.BlockSpec((2, d), const2),
            pl.BlockSpec((ATT_W, d), const2),
            pl.BlockSpec((S5_W, 2 * d), const2),
            pl.BlockSpec((GM_W, d), const2),
            pl.BlockSpec((d, d), const2),
            pl.BlockSpec((N_EXP, d), const2),
        ],
        out_specs=[pl.BlockSpec((1, tm, d), row), pl.BlockSpec((1, tm, d), row),
                   pl.BlockSpec((1, N_EXP, tm), lambda j, bb: (bb, 0, j))],
        out_shape=[jax.ShapeDtypeStruct((b, n, d), F32), jax.ShapeDtypeStruct((b, n, d), F32),
                   jax.ShapeDtypeStruct((b, N_EXP, n), F32)],
        compiler_params=_params(("arbitrary", "arbitrary")),
        name="merge_branches",
    )(x, o_att, y5, y5, ogm, gates, modv, ng12, watt, wglu, wgm, wout, wr_t)


def _cumsum_lanes(x, tri):
    outs = []
    run = jnp.zeros((x.shape[0], 1), F32)
    for c in range(x.shape[1] // 128):
        blk = jnp.dot(x[:, c * 128:(c + 1) * 128].astype(BF16), tri, preferred_element_type=F32) + run
        outs.append(blk)
        run = blk[:, 127:128]
    return jnp.concatenate(outs, axis=1)


def _topk_body(cap, p_ref, idx_ref, cs_ref):
    p = p_ref[0]
    ne, n = p.shape
    bits = pltpu.bitcast(p, jnp.int32)
    thr = jnp.zeros((ne, 1), jnp.int32)
    for bit in range(30, -1, -1):
        cand = thr | (1 << bit)
        cnt = jnp.sum(jnp.where(bits >= cand, 1.0, 0.0), axis=-1, keepdims=True)
        thr = jnp.where(cnt >= cap, cand, thr)
    gt = bits > thr
    eq = bits == thr
    n_gt = jnp.sum(jnp.where(gt, 1.0, 0.0), axis=-1, keepdims=True)
    r = lax.broadcasted_iota(jnp.int32, (128, 128), 0)
    c = lax.broadcasted_iota(jnp.int32, (128, 128), 1)
    tri = jnp.where(r <= c, 1.0, 0.0).astype(BF16)
    eq_f = jnp.where(eq, 1.0, 0.0)
    eq_rank = _cumsum_lanes(eq_f, tri) - eq_f
    sel = jnp.where(gt | (eq & (eq_rank < cap - n_gt)), 1.0, 0.0)
    cs_ref[...] = _cumsum_lanes(sel, tri)
    nblk = idx_ref.shape[2] // 128

    def per_expert(e, carry):
        row = cs_ref[pl.ds(e, 1), :]
        blks = []
        for sb in range(nblk):
            slot = (lax.broadcasted_iota(jnp.int32, (128, 1), 0) + sb * 128).astype(F32)
            cnt = jnp.sum(jnp.where(row <= slot, 1.0, 0.0), axis=-1, keepdims=True)
            blks.append(jnp.broadcast_to(cnt, (128, 128)).T[0:1, :])
        idx_ref[0, pl.ds(e, 1), :] = jnp.concatenate(blks, axis=1).astype(jnp.int32)
        return carry
    lax.fori_loop(0, ne, per_expert, 0)


def expert_topk(probs_t, cap):
    b, ne, n = probs_t.shape
    cap_pad = -(-cap // 128) * 128
    return pl.pallas_call(
        functools.partial(_topk_body, cap),
        grid=(b,),
        in_specs=[pl.BlockSpec((1, ne, n), lambda bb: (bb, 0, 0))],
        out_specs=pl.BlockSpec((1, ne, cap_pad), lambda bb: (bb, 0, 0)),
        out_shape=jax.ShapeDtypeStruct((b, ne, cap_pad), jnp.int32),
        scratch_shapes=[pltpu.VMEM((ne, n), F32)],
        compiler_params=_params(("arbitrary",)),
        name="expert_topk",
    )(probs_t)


def _moe_body(cap, nbp, idxa_ref, idxb_ref, idxn_ref, h2_hbm, wr_ref, wg_ref, wu_ref, wd_ref, ye_ref,
              xa_ref, xb_ref, sem):
    e = pl.program_id(0)
    bp = pl.program_id(1)
    step = e * nbp + bp
    last = pl.num_programs(0) * nbp - 1
    bp_next = jnp.minimum(step + 1, last) % nbp

    def row_copy(idx_smem, bb, buf, s, g, r):
        t = idx_smem[0, 0, g * 8 + r]
        return pltpu.make_async_copy(h2_hbm.at[bb, pl.ds(t, 1), :], buf.at[g, pl.ds(r, 1), :], sem.at[s])

    def issue_rolled(idx_smem, bb, buf, s):
        def body(g, carry):
            for r in range(8):
                row_copy(idx_smem, bb, buf, s, g, r).start()
            return carry
        lax.fori_loop(0, cap // 8, body, 0)

    def issue_inline(idx_smem, bb, buf, s):
        for g in range(cap // 8):
            for r in range(8):
                row_copy(idx_smem, bb, buf, s, g, r).start()

    def wait_all(buf, s):
        pltpu.make_async_copy(buf, buf, sem.at[s]).wait()

    def compute(buf, out_i):
        xe = buf[...].reshape(cap, buf.shape[-1]).astype(BF16)
        logits = jnp.dot(xe, wr_ref[...], preferred_element_type=F32)
        pe = jnp.exp(logits - jnp.max(logits, axis=-1, keepdims=True))
        probs = pe / jnp.sum(pe, axis=-1, keepdims=True)
        col = lax.broadcasted_iota(jnp.int32, probs.shape, 1)
        gate = jnp.sum(jnp.where(col == e, probs, 0.0), axis=-1, keepdims=True)
        ye = jnp.zeros((cap, xe.shape[1]), F32)
        for f0 in range(0, wg_ref.shape[2], FF_CHUNK):
            hg = jnp.dot(xe, wg_ref[0, :, f0:f0 + FF_CHUNK], preferred_element_type=F32)
            hu = jnp.dot(xe, wu_ref[0, :, f0:f0 + FF_CHUNK], preferred_element_type=F32)
            hid = (hg * jax.nn.sigmoid(hg) * hu).astype(BF16)
            ye = ye + jnp.dot(hid, wd_ref[0, f0:f0 + FF_CHUNK, :], preferred_element_type=F32)
        ye_ref[out_i, 0] = ye * gate

    @pl.when(step == 0)
    def _():
        issue_rolled(idxa_ref, 2 * bp, xa_ref, 0)

    wait_all(xa_ref, 0)
    issue_inline(idxb_ref, 2 * bp + 1, xb_ref, 1)
    compute(xa_ref, 0)
    wait_all(xb_ref, 1)
    issue_inline(idxn_ref, 2 * bp_next, xa_ref, 0)
    compute(xb_ref, 1)

    @pl.when(step == last)
    def _():
        wait_all(xa_ref, 0)


def moe_experts(h2, idx, cap, wr, wg, wu, wd):
    b, n, d = h2.shape
    ne, cap_pad = idx.shape[1:]
    ff = wg.shape[-1]
    assert b % 2 == 0 and cap % 8 == 0
    nbp = b // 2
    idx3 = idx.reshape(b * ne, 1, cap_pad)

    def idx_a(e, bp):
        return (2 * bp * ne + e, 0, 0)

    def idx_b(e, bp):
        return ((2 * bp + 1) * ne + e, 0, 0)

    def idx_next(e, bp):
        step = jnp.minimum(e * nbp + bp + 1, ne * nbp - 1)
        return (2 * (step % nbp) * ne + step // nbp, 0, 0)

    once = pl.Buffered(1)
    return pl.pallas_call(
        functools.partial(_moe_body, cap, nbp),
        grid=(ne, nbp),
        in_specs=[
            pl.BlockSpec((1, 1, cap_pad), idx_a, memory_space=pltpu.SMEM),
            pl.BlockSpec((1, 1, cap_pad), idx_b, memory_space=pltpu.SMEM),
            pl.BlockSpec((1, 1, cap_pad), idx_next, memory_space=pltpu.SMEM),
            pl.BlockSpec(memory_space=pl.ANY),
            pl.BlockSpec((d, ne), lambda e, bp: (0, 0)),
            pl.BlockSpec((1, d, ff), lambda e, bp: (e, 0, 0), pipeline_mode=once),
            pl.BlockSpec((1, d, ff), lambda e, bp: (e, 0, 0), pipeline_mode=once),
            pl.BlockSpec((1, ff, d), lambda e, bp: (e, 0, 0), pipeline_mode=once),
        ],
        out_specs=pl.BlockSpec((2, 1, cap, d), lambda e, bp: (bp, e, 0, 0)),
        out_shape=jax.ShapeDtypeStruct((b, ne, cap, d), F32),
        scratch_shapes=[pltpu.VMEM((cap // 8, 8, d), F32), pltpu.VMEM((cap // 8, 8, d), F32),
                        pltpu.SemaphoreType.DMA((2,))],
        compiler_params=_params(("arbitrary", "arbitrary")),
        name="moe_experts",
    )(idx3, idx3, idx3, h2, wr, wg, wu, wd)


def _combine_body(cap, idx_ref, ye_ref, acc_ref):
    e = pl.program_id(1)

    @pl.when(e == 0)
    def _():
        acc_ref[...] = jnp.zeros(acc_ref.shape, F32)

    def body(g, carry):
        ts = [idx_ref[0, 0, g * COMBINE_GROUP + r] for r in range(COMBINE_GROUP)]
        rows = [acc_ref[0, pl.ds(ts[r], 1), :] + ye_ref[0, 0, pl.ds(g * COMBINE_GROUP + r, 1), :]
                for r in range(COMBINE_GROUP)]
        for r in range(COMBINE_GROUP):
            acc_ref[0, pl.ds(ts[r], 1), :] = rows[r]
        return carry
    lax.fori_loop(0, cap // COMBINE_GROUP, body, 0)


def moe_combine(ye, idx, n):
    b, ne, cap, d = ye.shape
    cap_pad = idx.shape[2]
    idx3 = idx.reshape(b * ne, 1, cap_pad)
    assert n * d * 4 <= V7X_VMEM_BYTES // 2
    return pl.pallas_call(
        functools.partial(_combine_body, cap),
        grid=(b, ne),
        in_specs=[
            pl.BlockSpec((1, 1, cap_pad), lambda bb, e: (bb * ne + e, 0, 0), memory_space=pltpu.SMEM),
            pl.BlockSpec((1, 1, cap, d), lambda bb, e: (bb, e, 0, 0)),
        ],
        out_specs=pl.BlockSpec((1, n, d), lambda bb, e: (bb, 0, 0), pipeline_mode=pl.Buffered(1)),
        out_shape=jax.ShapeDtypeStruct((b, n, d), F32),
        compiler_params=_params(("arbitrary", "arbitrary")),
        name="moe_combine",
    )(idx3, ye)


def _resid_body(x_ref, y_ref, mod_ref, ng_ref, o_ref):
    y = y_ref[0]
    yn = y * lax.rsqrt(jnp.mean(y * y, axis=-1, keepdims=True) + RMS_EPS) * ng_ref[...]
    o_ref[0] = x_ref[0] + mod_ref[0, 5:6, :] * yn


def moe_residual(x, y, modv, ng3, tm):
    b, n, d = x.shape
    row = lambda bb, j: (bb, j, 0)
    return pl.pallas_call(
        _resid_body,
        grid=(b, n // tm),
        in_specs=[pl.BlockSpec((1, tm, d), row), pl.BlockSpec((1, tm, d), row),
                  pl.BlockSpec((1, 6, d), lambda bb, j: (bb, 0, 0)), pl.BlockSpec((1, d), lambda bb, j: (0, 0))],
        out_specs=pl.BlockSpec((1, tm, d), row),
        out_shape=jax.ShapeDtypeStruct((b, n, d), F32),
        compiler_params=_params(("arbitrary", "arbitrary")),
        name="moe_residual",
    )(x, y, modv, ng3)


def _rope_tables(n):
    rows = n // GRID_W
    row = jnp.repeat(jnp.arange(rows), GRID_W).astype(F32)
    col = jnp.tile(jnp.arange(GRID_W), rows).astype(F32)
    inv = ROPE_THETA ** (-jnp.arange(ROPE_FREQS, dtype=F32) / ROPE_FREQS)
    ar, ac = row[:, None] * inv, col[:, None] * inv
    cos64 = jnp.concatenate([jnp.cos(ar), jnp.cos(ar), jnp.cos(ac), jnp.cos(ac)], axis=1)
    sin64 = jnp.concatenate([-jnp.sin(ar), jnp.sin(ar), -jnp.sin(ac), jnp.sin(ac)], axis=1)
    return jnp.tile(cos64, (1, 2)), jnp.tile(sin64, (1, 2))


def _moe_sublayer(x_mid, h2, probs_t, modv, ng3, wr, wg, wu, wd, tm):
    n = x_mid.shape[1]
    cap = CAP_FACTOR * n // N_EXP
    idx = expert_topk(probs_t, cap)
    ye = moe_experts(h2, idx, cap, wr, wg, wu, wd)
    y = moe_combine(ye, idx, n)
    return moe_residual(x_mid, y, modv, ng3, tm)


def _forward(x, c, ctx, c_ctx, w_mod, b_mod, norm_g, w_in, diff_lambda, diff_subln_g, s5_lam_re, s5_lam_im,
             s5_log_dt, s5_b_re, s5_b_im, s5_c_re, s5_c_im, s5_d, w_s5_glu, gmlp_norm_g, gmlp_ws, gmlp_bs,
             w_br_att, w_br_gmlp, w_out, w_router, w_e_gate, w_e_up, w_e_down, *, tm, tq, tk, t_chunk, hpb):
    b, n, d = x.shape
    n_ctx = ctx.shape[1]
    depth = w_mod.shape[0]
    tm_c = min(tm, n_ctx)
    tq_c = min(tq, n_ctx)

    c_rows = jnp.zeros((MOD_ROWS, d), F32).at[:b].set(c).at[b].set(c_ctx)
    mod_all = modulation(c_rows, w_mod, b_mod).reshape(depth, MOD_ROWS, 6, d)
    cos_t, sin_t = _rope_tables(n)
    cos_c, sin_c = jnp.ones((n_ctx, 128), F32), jnp.zeros((n_ctx, 128), F32)

    xc = ctx
    for l in range(depth):
        last = l == depth - 1
        lam_init = 0.8 - 0.6 * math.exp(-0.3 * l)
        modv = mod_all[l, :b]
        modc = jnp.broadcast_to(mod_all[l, b:b + 1], (b, 6, d))
        ng0 = norm_g[l, 0:1]
        w_in_b = w_in[l].astype(BF16)
        gg = gmlp_norm_g[l][None, :]
        ws = gmlp_ws[l].astype(BF16)
        bs = jnp.repeat(gmlp_bs[l].T, GM_HD, axis=1)
        sg = diff_subln_g[l][None, :]
        wr = w_router[l].astype(BF16)
        wg, wu, wd = w_e_gate[l].astype(BF16), w_e_up[l].astype(BF16), w_e_down[l].astype(BF16)

        q, k, v, u, ogm, gates = input_projection(x, modv, ng0, w_in_b, cos_t, sin_t, gg, ws, bs, True, tm)
        pc = input_projection(xc, modc, ng0, w_in_b, cos_c, sin_c, gg, ws, bs, not last, tm_c)
        if last:
            kc, vc, uc = pc
        else:
            qc, kc, vc, uc, ogm_c, gates_c = pc

        o_att = diff_attention(q, kc, vc, k, v, diff_lambda[l], sg, lam_init, tq, tk, hpb)

        wb, a, wc = s5_weights(s5_lam_re[l], s5_lam_im[l], s5_log_dt[l], s5_b_re[l], s5_b_im[l],
                               s5_c_re[l], s5_c_im[l])
        y5 = s5_scan(u, uc, b, wb, a, wc, s5_d[l][None, :], min(t_chunk, n_ctx))

        merge_w = (norm_g[l, 1:3], w_br_att[l].astype(BF16), w_s5_glu[l].astype(BF16),
                   w_br_gmlp[l].astype(BF16), w_out[l].astype(BF16), w_router[l].T.astype(BF16))
        x_mid, h2, probs_t = merge_branches(x, o_att, y5, 0, ogm, gates, modv, *merge_w, tm)
        x = _moe_sublayer(x_mid, h2, probs_t, modv, norm_g[l, 3:4], wr, wg, wu, wd, tm)
        if not last:
            o_att_c = diff_attention(qc, kc, vc, None, None, diff_lambda[l], sg, lam_init, tq_c, tk, hpb)
            xc_mid, h2c, probs_c = merge_branches(xc, o_att_c, y5, n, ogm_c, gates_c, modc, *merge_w, tm_c)
            xc = _moe_sublayer(xc_mid, h2c, probs_c, modc, norm_g[l, 3:4], wr, wg, wu, wd, tm_c)
    return x


def kernel(x, c, ctx, c_ctx, w_mod, b_mod, norm_g, w_in, diff_lambda, diff_subln_g, s5_lam_re, s5_lam_im, s5_log_dt, s5_b_re, s5_b_im, s5_c_re, s5_c_im, s5_d, w_s5_glu, gmlp_norm_g, gmlp_ws, gmlp_bs, w_br_att, w_br_gmlp, w_out, w_router, w_e_gate, w_e_up, w_e_down):
    return _forward(x, c, ctx, c_ctx, w_mod, b_mod, norm_g, w_in, diff_lambda, diff_subln_g, s5_lam_re, s5_lam_im,
                    s5_log_dt, s5_b_re, s5_b_im, s5_c_re, s5_c_im, s5_d, w_s5_glu, gmlp_norm_g, gmlp_ws, gmlp_bs,
                    w_br_att, w_br_gmlp, w_out, w_router, w_e_gate, w_e_up, w_e_down,
                    tm=512, tq=512, tk=512, t_chunk=256, hpb=4)
```

```python
import functools
import math

import jax
import jax.numpy as jnp
from jax import lax
from jax.experimental import pallas as pl
from jax.experimental.pallas import tpu as pltpu

F32 = jnp.float32
BF16 = jnp.bfloat16

D_MODEL = 1024
GRID_W = 64
RMS_EPS = 1e-6
N_HEADS = 4
HEAD_DIM = 64
V_DIM = 2 * HEAD_DIM
ATT_W = N_HEADS * V_DIM
ROPE_THETA = 10000.0
ROPE_FREQS = HEAD_DIM // 4
S5_W = D_MODEL // 4
S5_P = 16
S5_G = S5_W // S5_P
S5_N = 64
S5_LANES = S5_G * S5_N
GM_W = D_MODEL // 4
GM_H = 4
GM_HD = GM_W // GM_H
CHUNK = 128
N_EXP = 16
CAP_FACTOR = 2
K_END = ATT_W
V_END = K_END + ATT_W
S5_END = V_END + S5_W
Q_END = S5_END + ATT_W
GMLP_END = Q_END + 2 * GM_W
IN_COLS = GMLP_END + 3 * D_MODEL
LOG2E = 1.4426950408889634

V7X_VMEM_BYTES = 64 * 1024 * 1024
VMEM_LIMIT = 56 * 1024 * 1024
FF_CHUNK = 512
MOE_ROWS_PER_STEP = 1024
S5_PARTS = 4
COMBINE_X_ROWS = 512
COMBINE_GROUP = 4
ATT_PIECE = 256
VT_ROWS = V_DIM + 16
MOD_ROWS = 16


def _params(sem):
    return pltpu.CompilerParams(dimension_semantics=sem, vmem_limit_bytes=VMEM_LIMIT)


def _mod_body(c_ref, w_ref, b_ref, o_ref):
    c = c_ref[...]
    s = (c * jax.nn.sigmoid(c)).astype(BF16)
    o_ref[0] = jnp.dot(s, w_ref[0].astype(BF16), preferred_element_type=F32) + b_ref[0]


def modulation(c_rows, w_mod, b_mod, tn=1536):
    depth, d, n6 = w_mod.shape
    return pl.pallas_call(
        _mod_body,
        grid=(depth, n6 // tn),
        in_specs=[
            pl.BlockSpec((MOD_ROWS, d), lambda l, j: (0, 0)),
            pl.BlockSpec((1, d, tn), lambda l, j: (l, 0, j)),
            pl.BlockSpec((1, 1, tn), lambda l, j: (l, 0, j)),
        ],
        out_specs=pl.BlockSpec((1, MOD_ROWS, tn), lambda l, j: (l, 0, j)),
        out_shape=jax.ShapeDtypeStruct((depth, MOD_ROWS, n6), F32),
        compiler_params=_params(("arbitrary", "arbitrary")),
        name="modulation",
    )(c_rows, w_mod, b_mod.reshape(depth, 1, n6))


def _rope_cols(p, cos, sin):
    lane = lax.broadcasted_iota(jnp.int32, (p.shape[0], 128), 1)
    first = (lane % 32) < ROPE_FREQS
    outs = []
    for s in range(p.shape[1] // 128):
        ps = p[:, s * 128:(s + 1) * 128]
        sw = jnp.where(first, pltpu.roll(ps, 128 - ROPE_FREQS, 1), pltpu.roll(ps, ROPE_FREQS, 1))
        outs.append(ps * cos + sw * sin)
    return jnp.concatenate(outs, axis=1)


def _inproj_body(full, x_ref, mod_ref, ng_ref, w_ref, cos_ref, sin_ref, gg_ref, ws_ref, bs_ref, *outs):
    xt = x_ref[0]
    h = xt * lax.rsqrt(jnp.mean(xt * xt, axis=-1, keepdims=True) + RMS_EPS) * ng_ref[...]
    h = h * (1.0 + mod_ref[0, 1:2, :]) + mod_ref[0, 0:1, :]
    hb = h.astype(BF16)

    def proj(a, b):
        return jnp.dot(hb, w_ref[:, a:b], preferred_element_type=F32)

    cos = cos_ref[...]
    sin = sin_ref[...]
    if full:
        q_ref, k_ref, v_ref, u_ref, ogm_ref, gate_ref = outs
    else:
        k_ref, v_ref, u_ref = outs
    k_ref[0] = _rope_cols(proj(0, K_END), cos, sin).astype(BF16)
    vt = proj(K_END, V_END).T
    pad_row = lax.broadcasted_iota(jnp.int32, (VT_ROWS - V_DIM, vt.shape[1]), 0)
    ones_pad = jnp.where(pad_row == 0, 1.0, 0.0).astype(BF16)
    for hh in range(N_HEADS):
        v_ref[0, hh * VT_ROWS:hh * VT_ROWS + V_DIM, :] = vt[hh * V_DIM:(hh + 1) * V_DIM, :].astype(BF16)
        v_ref[0, hh * VT_ROWS + V_DIM:(hh + 1) * VT_ROWS, :] = ones_pad
    nb = u_ref.shape[1] // xt.shape[0]
    pu = proj(V_END, S5_END)
    for s in range(S5_W // 128):
        u_ref[s, pl.ds(pl.program_id(1), xt.shape[0], stride=nb), :] = pu[:, s * 128:(s + 1) * 128]
    if not full:
        return
    q_ref[0] = (_rope_cols(proj(S5_END, Q_END), cos, sin) * (HEAD_DIM ** -0.5 * LOG2E)).astype(BF16)

    zg = jax.nn.gelu(proj(Q_END, GMLP_END))
    zu = zg[:, :GM_W]
    zv = zg[:, GM_W:]
    mu = jnp.mean(zv, axis=-1, keepdims=True)
    zc = zv - mu
    zvn = (zc * lax.rsqrt(jnp.mean(zc * zc, axis=-1, keepdims=True) + RMS_EPS) * gg_ref[...]).astype(BF16)
    head = lax.broadcasted_iota(jnp.int32, (CHUNK, GM_W), 1) // GM_HD
    tm = xt.shape[0]
    for c in range(tm // CHUNK):
        zc_blk = zvn[c * CHUNK:(c + 1) * CHUNK, :]
        mixed = bs_ref[...]
        for g in range(GM_H):
            mixed = mixed + jnp.dot(ws_ref[g], jnp.where(head == g, zc_blk, jnp.zeros_like(zc_blk)),
                                    preferred_element_type=F32)
        ogm_ref[0, c * CHUNK:(c + 1) * CHUNK, :] = (zu[c * CHUNK:(c + 1) * CHUNK, :] * mixed).astype(BF16)

    for j in range(3):
        a = GMLP_END + j * D_MODEL
        gate_ref[0, :, j * D_MODEL:(j + 1) * D_MODEL] = jax.nn.sigmoid(proj(a, a + D_MODEL)).astype(BF16)


def input_projection(x, modv, ng, w_in, cos_t, sin_t, gg, ws, bs, full, tm):
    b, n, d = x.shape
    assert n % tm == 0 and tm % CHUNK == 0
    row = lambda j, bb: (bb, j, 0)
    const2 = lambda j, bb: (0, 0)
    out_shape = [jax.ShapeDtypeStruct((b, n, ATT_W), BF16),
                 jax.ShapeDtypeStruct((b, N_HEADS * VT_ROWS, n), BF16),
                 jax.ShapeDtypeStruct((S5_W // 128, n * b, 128), F32)]
    out_specs = [pl.BlockSpec((1, tm, ATT_W), row),
                 pl.BlockSpec((1, N_HEADS * VT_ROWS, tm), lambda j, bb: (bb, 0, j)),
                 pl.BlockSpec((S5_W // 128, tm * b, 128), lambda j, bb: (0, j, 0))]
    if full:
        out_shape = [jax.ShapeDtypeStruct((b, n, ATT_W), BF16)] + out_shape + [
            jax.ShapeDtypeStruct((b, n, GM_W), BF16), jax.ShapeDtypeStruct((b, n, 3 * d), BF16)]
        out_specs = [pl.BlockSpec((1, tm, ATT_W), row)] + out_specs + [
            pl.BlockSpec((1, tm, GM_W), row), pl.BlockSpec((1, tm, 3 * d), row)]
    in_specs = [
        pl.BlockSpec((1, tm, d), row),
        pl.BlockSpec((1, 6, d), lambda j, bb: (bb, 0, 0)),
        pl.BlockSpec((1, d), const2),
        pl.BlockSpec((d, IN_COLS), const2),
        pl.BlockSpec((tm, 128), lambda j, bb: (j, 0)),
        pl.BlockSpec((tm, 128), lambda j, bb: (j, 0)),
        pl.BlockSpec((1, GM_W), const2),
        pl.BlockSpec((GM_H, CHUNK, CHUNK), lambda j, bb: (0, 0, 0)),
        pl.BlockSpec((CHUNK, GM_W), const2),
    ]
    return pl.pallas_call(
        functools.partial(_inproj_body, full),
        grid=(n // tm, b),
        in_specs=in_specs,
        out_specs=out_specs,
        out_shape=out_shape,
        compiler_params=_params(("arbitrary", "arbitrary")),
        name="input_projection" if full else "input_projection_ctx_side",
    )(x, modv, ng, w_in, cos_t, sin_t, gg, ws, bs)


def _attn_body(lam_init, tk, has_latent, q_ref, kc_ref, vtc_ref, k_ref, vt_ref, dl_ref, sg_ref, o_ref,
               m_ref, acc_ref, q2_ref, s_ref, p_ref, al_ref, bm_ref):
    tq = q_ref.shape[1]
    nc = kc_ref.shape[1]
    n_strip = 2 * tq // 128
    heads = range(q_ref.shape[2] // V_DIM)

    def hl(h):
        return slice(h * V_DIM, (h + 1) * V_DIM)

    def hr(h):
        return slice(h * VT_ROWS, (h + 1) * VT_ROWS)

    for h in heads:
        q = q_ref[0, :, hl(h)]
        lane = lax.broadcasted_iota(jnp.int32, q.shape, 1)
        zero = jnp.zeros_like(q)
        q2_ref[h, 0:tq, :] = jnp.where(lane < HEAD_DIM, q, zero)
        q2_ref[h, tq:2 * tq, :] = jnp.where(lane >= HEAD_DIM, q, zero)
    m_ref[...] = jnp.full(m_ref.shape, -jnp.inf, F32)
    acc_ref[...] = jnp.zeros(acc_ref.shape, F32)

    def qk(h, kb, slot, rows):
        st = lax.dot_general(kb, q2_ref[h], (((1,), (1,)), ((), ())), preferred_element_type=F32)
        for c in range(n_strip):
            s_ref[h, slot, c, 0:rows, :] = st[:, c * 128:(c + 1) * 128]
        bm_ref[h, slot] = jnp.max(st, axis=0, keepdims=True)

    def sm(h, slot, rows):
        m_old = m_ref[h]
        m_new = jnp.maximum(m_old, bm_ref[h, slot])
        al_ref[h, slot] = jnp.exp2(m_old - m_new)
        m_ref[h] = m_new
        piece = min(ATT_PIECE, rows)
        for c in range(n_strip):
            for r0 in range(0, rows, piece):
                p_ref[h, slot, c, r0:r0 + piece, :] = jnp.exp2(
                    s_ref[h, slot, c, r0:r0 + piece, :] - m_new[:, c * 128:(c + 1) * 128]).astype(BF16)

    def pv(h, vtb, slot, rows):
        p = jnp.concatenate([p_ref[h, slot, c, 0:rows, :] for c in range(n_strip)], axis=1)
        acc_ref[h] = al_ref[h, slot] * acc_ref[h] + jnp.dot(vtb, p, preferred_element_type=F32)

    def k_blk(h, j):
        return k_ref[0, pl.ds(pl.multiple_of(j * tk, tk), tk), hl(h)]

    def vt_blk(h, j):
        return vt_ref[0, hr(h), pl.ds(pl.multiple_of(j * tk, tk), tk)]

    for h in heads:
        qk(h, kc_ref[0, :, hl(h)], 0, nc)
        sm(h, 0, nc)
    if not has_latent:
        for h in heads:
            pv(h, vtc_ref[0, hr(h), :], 0, nc)
    else:
        n_kv = k_ref.shape[1] // tk
        assert n_kv >= 4 and n_kv % 2 == 0
        for h in heads:
            qk(h, k_blk(h, 0), 1, tk)
            pv(h, vtc_ref[0, hr(h), :], 0, nc)
            qk(h, k_blk(h, 1), 0, tk)
            sm(h, 1, tk)

        def body(i, carry):
            for h in heads:
                pv(h, vt_blk(h, 2 * i), 1, tk)
                qk(h, k_blk(h, 2 * i + 2), 1, tk)
                sm(h, 0, tk)
                sm(h, 1, tk)
                pv(h, vt_blk(h, 2 * i + 1), 0, tk)
                qk(h, k_blk(h, 2 * i + 3), 0, tk)
            return carry
        lax.fori_loop(0, (n_kv - 2) // 2, body, 0)
        for h in heads:
            pv(h, vt_blk(h, n_kv - 2), 1, tk)
            sm(h, 0, tk)
            pv(h, vt_blk(h, n_kv - 1), 0, tk)

    dl = dl_ref[...]
    lam = (jnp.exp(jnp.sum(dl[0:1, :] * dl[1:2, :], axis=-1, keepdims=True))
           - jnp.exp(jnp.sum(dl[2:3, :] * dl[3:4, :], axis=-1, keepdims=True)) + lam_init)
    for h in heads:
        num = acc_ref[h, 0:V_DIM, :]
        den = acc_ref[h, V_DIM:V_DIM + 1, :]
        ot = num[:, 0:tq] / den[:, 0:tq] - lam * (num[:, tq:2 * tq] / den[:, tq:2 * tq])
        ot = ot * lax.rsqrt(jnp.mean(ot * ot, axis=0, keepdims=True) + RMS_EPS)
        o_ref[0, :, hl(h)] = (ot.T * sg_ref[...] * (1.0 - lam_init)).astype(BF16)


def diff_attention(q, kc, vtc, k, vt, dlam, sg, lam_init, tq, tk, hpb):
    b, nq, _ = q.shape
    nc = kc.shape[1]
    has_latent = k is not None
    if not has_latent:
        k, vt = kc, vtc
    n = k.shape[1]
    assert nq % tq == 0 and (not has_latent or n % tk == 0) and N_HEADS % hpb == 0
    head_rows = lambda bb, h, i: (bb, i, h)
    head_all = lambda bb, h, i: (bb, 0, h)
    head_all_t = lambda bb, h, i: (bb, h, 0)
    kmax = max(tk, nc)
    return pl.pallas_call(
        functools.partial(_attn_body, lam_init, tk, has_latent),
        grid=(b, N_HEADS // hpb, nq // tq),
        in_specs=[
            pl.BlockSpec((1, tq, hpb * V_DIM), head_rows),
            pl.BlockSpec((1, nc, hpb * V_DIM), head_all),
            pl.BlockSpec((1, hpb * VT_ROWS, nc), head_all_t),
            pl.BlockSpec((1, n, hpb * V_DIM), head_all, pipeline_mode=pl.Buffered(1)),
            pl.BlockSpec((1, hpb * VT_ROWS, n), head_all_t, pipeline_mode=pl.Buffered(1)),
            pl.BlockSpec((4, HEAD_DIM), lambda bb, h, i: (0, 0)),
            pl.BlockSpec((1, V_DIM), lambda bb, h, i: (0, 0)),
        ],
        out_specs=pl.BlockSpec((1, tq, hpb * V_DIM), head_rows),
        out_shape=jax.ShapeDtypeStruct((b, nq, ATT_W), BF16),
        scratch_shapes=[pltpu.VMEM((hpb, 1, 2 * tq), F32), pltpu.VMEM((hpb, VT_ROWS, 2 * tq), F32),
                        pltpu.VMEM((hpb, 2 * tq, V_DIM), BF16),
                        pltpu.VMEM((hpb, 2, 2 * tq // 128, kmax, 128), F32),
                        pltpu.VMEM((hpb, 2, 2 * tq // 128, kmax, 128), BF16),
                        pltpu.VMEM((hpb, 2, 1, 2 * tq), F32), pltpu.VMEM((hpb, 2, 1, 2 * tq), F32)],
        compiler_params=_params(("arbitrary", "arbitrary", "arbitrary")),
        name="diff_attention" if has_latent else "diff_attention_ctx",
    )(q, kc, vtc, k, vt, dlam, sg)


def _s5_body(t_chunk, nb, cctx, ul_ref, uc_ref, wb_ref, a_ref, wc_ref, dsk_ref, y_ref, xs_ref, st_ref):
    d = pl.program_id(0)
    kk = pl.program_id(1)
    n_slab = ul_ref.shape[0]

    @pl.when(kk == 0)
    def _():
        st_ref[...] = jnp.zeros(st_ref.shape, F32)

    tp = t_chunk // S5_PARTS
    from_ctx = kk < cctx

    def u_rows(part):
        rs = slice(part * tp * nb, (part + 1) * tp * nb)
        return jnp.concatenate([jnp.where(from_ctx, uc_ref[s, rs, :], ul_ref[s, rs, :]) for s in range(n_slab)],
                               axis=1)

    for part in range(S5_PARTS):
        xs_ref[part * tp:(part + 1) * tp] = jnp.dot(u_rows(part).astype(BF16), wb_ref[0],
                                                    preferred_element_type=F32).reshape(tp, nb, 2 * S5_LANES)
    ar = jnp.broadcast_to(a_ref[0, 0:1, :], (nb, S5_LANES))
    ai = jnp.broadcast_to(a_ref[0, 1:2, :], (nb, S5_LANES))

    def body(i, carry):
        sr, si = carry
        t = jnp.where(d == 0, i, t_chunk - 1 - i)
        xr = xs_ref[t, :, 0:S5_LANES]
        xi = xs_ref[t, :, S5_LANES:2 * S5_LANES]
        nr = ar * sr - ai * si + xr
        ni = ar * si + ai * sr + xi
        xs_ref[t, :, 0:S5_LANES] = nr
        xs_ref[t, :, S5_LANES:2 * S5_LANES] = ni
        return nr, ni

    sr, si = lax.fori_loop(0, t_chunk, body, (st_ref[0], st_ref[1]), unroll=4)
    st_ref[0] = sr
    st_ref[1] = si
    skip = jnp.where(d == 0, 1.0, 0.0) * dsk_ref[...]
    for part in range(S5_PARTS):
        y = jnp.dot(xs_ref[part * tp:(part + 1) * tp].reshape(tp * nb, 2 * S5_LANES).astype(BF16), wc_ref[0],
                    preferred_element_type=F32)
        y = y + skip * u_rows(part)
        for s in range(n_slab):
            y_ref[0, s, part * tp * nb:(part + 1) * tp * nb, :] = y[:, s * 128:(s + 1) * 128]


def s5_scan(u_lat, u_ctx, nb, wb, a, wc, dskip, t_chunk):
    n_slab, rows_lat, _ = u_lat.shape
    rows = rows_lat + u_ctx.shape[1]
    assert rows_lat % (t_chunk * nb) == 0 and u_ctx.shape[1] % (t_chunk * nb) == 0
    nchunk = rows // (t_chunk * nb)
    nlat = rows_lat // (t_chunk * nb)
    cctx = nchunk - nlat

    def chunk_index(d, kk):
        fwd = jnp.where(kk < cctx, nlat + kk, kk - cctx)
        return jnp.where(d == 0, fwd, nchunk - 1 - kk)

    def lat_index(d, kk):
        return jnp.clip(chunk_index(d, kk), 0, nlat - 1)

    def ctx_index(d, kk):
        return jnp.clip(chunk_index(d, kk) - nlat, 0, cctx - 1)

    return pl.pallas_call(
        functools.partial(_s5_body, t_chunk, nb, cctx),
        grid=(2, nchunk),
        in_specs=[
            pl.BlockSpec((n_slab, t_chunk * nb, 128), lambda d, kk: (0, lat_index(d, kk), 0)),
            pl.BlockSpec((n_slab, t_chunk * nb, 128), lambda d, kk: (0, ctx_index(d, kk), 0)),
            pl.BlockSpec((1, S5_W, 2 * S5_LANES), lambda d, kk: (d, 0, 0)),
            pl.BlockSpec((1, 2, S5_LANES), lambda d, kk: (d, 0, 0)),
            pl.BlockSpec((1, 2 * S5_LANES, S5_W), lambda d, kk: (d, 0, 0)),
            pl.BlockSpec((1, S5_W), lambda d, kk: (0, 0)),
        ],
        out_specs=pl.BlockSpec((1, n_slab, t_chunk * nb, 128), lambda d, kk: (d, 0, chunk_index(d, kk), 0)),
        out_shape=jax.ShapeDtypeStruct((2, n_slab, rows, 128), F32),
        scratch_shapes=[pltpu.VMEM((t_chunk, nb, 2 * S5_LANES), F32), pltpu.VMEM((2, nb, S5_LANES), F32)],
        compiler_params=_params(("arbitrary", "arbitrary")),
        name="s5_scan",
    )(u_lat, u_ctx, wb, a, wc, dskip)


def s5_weights(lam_re, lam_im, log_dt, b_re, b_im, c_re, c_im):
    dt = jnp.exp(log_dt.astype(F32))[..., None]
    lr, li = lam_re.astype(F32), lam_im.astype(F32)
    mag = jnp.exp(lr * dt)
    ar, ai = mag * jnp.cos(li * dt), mag * jnp.sin(li * dt)
    den = lr * lr + li * li
    fr = ((ar - 1.0) * lr + ai * li) / den
    fi = (ai * lr - (ar - 1.0) * li) / den
    bbr = fr[..., None] * b_re - fi[..., None] * b_im
    bbi = fr[..., None] * b_im + fi[..., None] * b_re
    eye = jnp.eye(S5_G, dtype=F32)
    def in_map(bb):
        return jnp.einsum('dgnc,gh->dgchn', bb, eye).reshape(2, S5_W, S5_LANES)
    wb = jnp.concatenate([in_map(bbr), in_map(bbi)], axis=-1).astype(BF16)
    def out_map(cc):
        return jnp.einsum('dgcn,gh->dgnhc', cc, eye).reshape(2, S5_LANES, S5_W)
    wc = jnp.concatenate([out_map(c_re.astype(F32)), out_map(-c_im.astype(F32))], axis=1).astype(BF16)
    a = jnp.stack([ar.reshape(2, S5_LANES), ai.reshape(2, S5_LANES)], axis=1)
    return wb, a, wc


def _merge_body(x_ref, oatt_ref, yf_ref, yb_ref, ogm_ref, gate_ref, mod_ref, ng_ref, watt_ref, wglu_ref,
                wgm_ref, wout_ref, wr_ref, xo_ref, h2_ref, pr_ref):
    y_att = jnp.dot(oatt_ref[0], watt_ref[...], preferred_element_type=F32)
    tm = x_ref.shape[1]
    nb = yf_ref.shape[2] // tm
    rows = pl.ds(pl.program_id(1), tm, stride=nb)
    y5 = jnp.concatenate([yf_ref[0, s, rows, :] + yb_ref[0, s, rows, :] for s in range(yf_ref.shape[1])], axis=1)
    z = jnp.dot(jax.nn.gelu(y5).astype(BF16), wglu_ref[...], preferred_element_type=F32)
    y5g = z[:, :D_MODEL] * jax.nn.sigmoid(z[:, D_MODEL:])
    y_gm = jnp.dot(ogm_ref[0], wgm_ref[...], preferred_element_type=F32)
    merged = (gate_ref[0, :, 0:D_MODEL].astype(F32) * y_att
              + gate_ref[0, :, D_MODEL:2 * D_MODEL].astype(F32) * y5g
              + gate_ref[0, :, 2 * D_MODEL:3 * D_MODEL].astype(F32) * y_gm)
    mix = jnp.dot(merged.astype(BF16), wout_ref[...], preferred_element_type=F32)
    mixn = mix * lax.rsqrt(jnp.mean(mix * mix, axis=-1, keepdims=True) + RMS_EPS) * ng_ref[0:1, :]
    xn = x_ref[0] + mod_ref[0, 2:3, :] * mixn
    xo_ref[0] = xn
    h2 = xn * lax.rsqrt(jnp.mean(xn * xn, axis=-1, keepdims=True) + RMS_EPS) * ng_ref[1:2, :]
    h2 = h2 * (1.0 + mod_ref[0, 4:5, :]) + mod_ref[0, 3:4, :]
    h2_ref[0] = h2
    logits = lax.dot_general(wr_ref[...], h2.astype(BF16), (((1,), (1,)), ((), ())),
                             preferred_element_type=F32)
    pe = jnp.exp(logits - jnp.max(logits, axis=0, keepdims=True))
    pr_ref[0] = pe / jnp.sum(pe, axis=0, keepdims=True)


def merge_branches(x, o_att, y5, t_offset, ogm, gates, modv, ng12, watt, wglu, wgm, wout, wr_t, tm):
    b, n, d = x.shape
    n_slab = y5.shape[1]
    assert t_offset % tm == 0
    row = lambda j, bb: (bb, j, 0)
    const2 = lambda j, bb: (0, 0)
    return pl.pallas_call(
        _merge_body,
        grid=(n // tm, b),
        in_specs=[
            pl.BlockSpec((1, tm, d), row),
            pl.BlockSpec((1, tm, ATT_W), row),
            pl.BlockSpec((1, n_slab, tm * b, 128), lambda j, bb: (0, 0, t_offset // tm + j, 0)),
            pl.BlockSpec((1, n_slab, tm * b, 128), lambda j, bb: (1, 0, t_offset // tm + j, 0)),
            pl.BlockSpec((1, tm, GM_W), row),
            pl.BlockSpec((1, tm, 3 * d), row),
            pl.BlockSpec((1, 6, d), lambda j, bb: (bb, 0, 0)),
            pl.BlockSpec((2, d), const2),
            pl.BlockSpec((ATT_W, d), const2),
            pl.BlockSpec((S5_W, 2 * d), const2),
            pl.BlockSpec((GM_W, d), const2),
            pl.BlockSpec((d, d), const2),
            pl.BlockSpec((N_EXP, d), const2),
        ],
        out_specs=[pl.BlockSpec((1, tm, d), row), pl.BlockSpec((1, tm, d), row),
                   pl.BlockSpec((1, N_EXP, tm), lambda j, bb: (bb, 0, j))],
        out_shape=[jax.ShapeDtypeStruct((b, n, d), F32), jax.ShapeDtypeStruct((b, n, d), F32),
                   jax.ShapeDtypeStruct((b, N_EXP, n), F32)],
        compiler_params=_params(("arbitrary", "arbitrary")),
        name="merge_branches",
    )(x, o_att, y5, y5, ogm, gates, modv, ng12, watt, wglu, wgm, wout, wr_t)


def _cumsum_lanes(x, tri):
    outs = []
    run = jnp.zeros((x.shape[0], 1), F32)
    for c in range(x.shape[1] // 128):
        blk = jnp.dot(x[:, c * 128:(c + 1) * 128].astype(BF16), tri, preferred_element_type=F32) + run
        outs.append(blk)
        run = blk[:, 127:128]
    return jnp.concatenate(outs, axis=1)


def _topk_body(cap, p_ref, idx_ref, cs_ref):
    p = p_ref[0]
    ne, n = p.shape
    bits = pltpu.bitcast(p, jnp.int32)
    thr = jnp.zeros((ne, 1), jnp.int32)
    for bit in range(30, -1, -1):
        cand = thr | (1 << bit)
        cnt = jnp.sum(jnp.where(bits >= cand, 1.0, 0.0), axis=-1, keepdims=True)
        thr = jnp.where(cnt >= cap, cand, thr)
    gt = bits > thr
    eq = bits == thr
    n_gt = jnp.sum(jnp.where(gt, 1.0, 0.0), axis=-1, keepdims=True)
    r = lax.broadcasted_iota(jnp.int32, (128, 128), 0)
    c = lax.broadcasted_iota(jnp.int32, (128, 128), 1)
    tri = jnp.where(r <= c, 1.0, 0.0).astype(BF16)
    eq_f = jnp.where(eq, 1.0, 0.0)
    eq_rank = _cumsum_lanes(eq_f, tri) - eq_f
    sel = jnp.where(gt | (eq & (eq_rank < cap - n_gt)), 1.0, 0.0)
    cs_ref[...] = _cumsum_lanes(sel, tri)
    nblk = idx_ref.shape[2] // 128

    def per_expert(e, carry):
        row = cs_ref[pl.ds(e, 1), :]
        blks = []
        for sb in range(nblk):
            slot = (lax.broadcasted_iota(jnp.int32, (128, 1), 0) + sb * 128).astype(F32)
            cnt = jnp.sum(jnp.where(row <= slot, 1.0, 0.0), axis=-1, keepdims=True)
            blks.append(jnp.broadcast_to(cnt, (128, 128)).T[0:1, :])
        idx_ref[0, pl.ds(e, 1), :] = jnp.concatenate(blks, axis=1).astype(jnp.int32)
        return carry
    lax.fori_loop(0, ne, per_expert, 0)


def expert_topk(probs_t, cap):
    b, ne, n = probs_t.shape
    cap_pad = -(-cap // 128) * 128
    return pl.pallas_call(
        functools.partial(_topk_body, cap),
        grid=(b,),
        in_specs=[pl.BlockSpec((1, ne, n), lambda bb: (bb, 0, 0))],
        out_specs=pl.BlockSpec((1, ne, cap_pad), lambda bb: (bb, 0, 0)),
        out_shape=jax.ShapeDtypeStruct((b, ne, cap_pad), jnp.int32),
        scratch_shapes=[pltpu.VMEM((ne, n), F32)],
        compiler_params=_params(("arbitrary",)),
        name="expert_topk",
    )(probs_t)


def _expert_ffn(x_rows, e, wr_ref, wg_ref, wu_ref, wd_ref):
    xe = x_rows.astype(BF16)
    logits = jnp.dot(xe, wr_ref[...], preferred_element_type=F32)
    pe = jnp.exp(logits - jnp.max(logits, axis=-1, keepdims=True))
    probs = pe / jnp.sum(pe, axis=-1, keepdims=True)
    col = lax.broadcasted_iota(jnp.int32, probs.shape, 1)
    gate = jnp.sum(jnp.where(col == e, probs, 0.0), axis=-1, keepdims=True)
    ye = jnp.zeros(x_rows.shape, F32)
    for f0 in range(0, wg_ref.shape[2], FF_CHUNK):
        hg = jnp.dot(xe, wg_ref[0, :, f0:f0 + FF_CHUNK], preferred_element_type=F32)
        hu = jnp.dot(xe, wu_ref[0, :, f0:f0 + FF_CHUNK], preferred_element_type=F32)
        hid = (hg * jax.nn.sigmoid(hg) * hu).astype(BF16)
        ye = ye + jnp.dot(hid, wd_ref[0, f0:f0 + FF_CHUNK, :], preferred_element_type=F32)
    return ye * gate


def _moe_all_samples_body(cap, nb, idx_ref, h2_hbm, wr_ref, wg_ref, wu_ref, wd_ref, ye_ref, x_ref, sem):
    e = pl.program_id(0)
    groups = cap // 8
    for bb in range(nb):
        for g in range(groups):
            for r in range(8):
                t = idx_ref[0, bb, g * 8 + r]
                pltpu.make_async_copy(h2_hbm.at[bb, pl.ds(t, 1), :], x_ref.at[bb * groups + g, pl.ds(r, 1), :],
                                      sem.at[0]).start()
    pltpu.make_async_copy(x_ref, x_ref, sem.at[0]).wait()
    ye = _expert_ffn(x_ref[...].reshape(nb * cap, x_ref.shape[-1]), e, wr_ref, wg_ref, wu_ref, wd_ref)
    ye_ref[:, 0] = ye.reshape(nb, cap, x_ref.shape[-1])


def _moe_body(cap, nbp, idxa_ref, idxb_ref, idxn_ref, h2_hbm, wr_ref, wg_ref, wu_ref, wd_ref, ye_ref,
              xa_ref, xb_ref, sem):
    e = pl.program_id(0)
    bp = pl.program_id(1)
    step = e * nbp + bp
    last = pl.num_programs(0) * nbp - 1
    bp_next = jnp.minimum(step + 1, last) % nbp

    def row_copy(idx_smem, bb, buf, s, g, r):
        t = idx_smem[0, 0, g * 8 + r]
        return pltpu.make_async_copy(h2_hbm.at[bb, pl.ds(t, 1), :], buf.at[g, pl.ds(r, 1), :], sem.at[s])

    def issue_rolled(idx_smem, bb, buf, s):
        def body(g, carry):
            for r in range(8):
                row_copy(idx_smem, bb, buf, s, g, r).start()
            return carry
        lax.fori_loop(0, cap // 8, body, 0)

    def issue_inline(idx_smem, bb, buf, s):
        for g in range(cap // 8):
            for r in range(8):
                row_copy(idx_smem, bb, buf, s, g, r).start()

    def wait_all(buf, s):
        pltpu.make_async_copy(buf, buf, sem.at[s]).wait()

    def compute(buf, out_i):
        xe = buf[...].reshape(cap, buf.shape[-1])
        ye_ref[out_i, 0] = _expert_ffn(xe, e, wr_ref, wg_ref, wu_ref, wd_ref)

    @pl.when(step == 0)
    def _():
        issue_rolled(idxa_ref, 2 * bp, xa_ref, 0)

    wait_all(xa_ref, 0)
    issue_inline(idxb_ref, 2 * bp + 1, xb_ref, 1)
    compute(xa_ref, 0)
    wait_all(xb_ref, 1)
    issue_inline(idxn_ref, 2 * bp_next, xa_ref, 0)
    compute(xb_ref, 1)

    @pl.when(step == last)
    def _():
        wait_all(xa_ref, 0)


def moe_experts(h2, idx, cap, wr, wg, wu, wd, layer):
    b, n, d = h2.shape
    ne, cap_pad = idx.shape[1:]
    ff = wg.shape[-1]
    assert b % 2 == 0 and cap % 8 == 0
    if b * cap <= MOE_ROWS_PER_STEP:
        return pl.pallas_call(
            functools.partial(_moe_all_samples_body, cap, b),
            grid=(ne,),
            in_specs=[
                pl.BlockSpec((1, b, cap_pad), lambda e: (e, 0, 0), memory_space=pltpu.SMEM),
                pl.BlockSpec(memory_space=pl.ANY),
                pl.BlockSpec((d, ne), lambda e: (0, 0)),
                pl.BlockSpec((1, d, ff), lambda e: (layer * ne + e, 0, 0)),
                pl.BlockSpec((1, d, ff), lambda e: (layer * ne + e, 0, 0)),
                pl.BlockSpec((1, ff, d), lambda e: (layer * ne + e, 0, 0)),
            ],
            out_specs=pl.BlockSpec((b, 1, cap, d), lambda e: (0, e, 0, 0)),
            out_shape=jax.ShapeDtypeStruct((b, ne, cap, d), F32),
            scratch_shapes=[pltpu.VMEM((b * cap // 8, 8, d), F32), pltpu.SemaphoreType.DMA((1,))],
            compiler_params=_params(("arbitrary",)),
            name="moe_experts_all_samples",
        )(idx.transpose(1, 0, 2), h2, wr, wg, wu, wd)
    nbp = b // 2
    idx3 = idx.reshape(b * ne, 1, cap_pad)

    def idx_a(e, bp):
        return (2 * bp * ne + e, 0, 0)

    def idx_b(e, bp):
        return ((2 * bp + 1) * ne + e, 0, 0)

    def idx_next(e, bp):
        step = jnp.minimum(e * nbp + bp + 1, ne * nbp - 1)
        return (2 * (step % nbp) * ne + step // nbp, 0, 0)

    once = pl.Buffered(1)
    return pl.pallas_call(
        functools.partial(_moe_body, cap, nbp),
        grid=(ne, nbp),
        in_specs=[
            pl.BlockSpec((1, 1, cap_pad), idx_a, memory_space=pltpu.SMEM),
            pl.BlockSpec((1, 1, cap_pad), idx_b, memory_space=pltpu.SMEM),
            pl.BlockSpec((1, 1, cap_pad), idx_next, memory_space=pltpu.SMEM),
            pl.BlockSpec(memory_space=pl.ANY),
            pl.BlockSpec((d, ne), lambda e, bp: (0, 0)),
            pl.BlockSpec((1, d, ff), lambda e, bp: (layer * ne + e, 0, 0), pipeline_mode=once),
            pl.BlockSpec((1, d, ff), lambda e, bp: (layer * ne + e, 0, 0), pipeline_mode=once),
            pl.BlockSpec((1, ff, d), lambda e, bp: (layer * ne + e, 0, 0), pipeline_mode=once),
        ],
        out_specs=pl.BlockSpec((2, 1, cap, d), lambda e, bp: (bp, e, 0, 0)),
        out_shape=jax.ShapeDtypeStruct((b, ne, cap, d), F32),
        scratch_shapes=[pltpu.VMEM((cap // 8, 8, d), F32), pltpu.VMEM((cap // 8, 8, d), F32),
                        pltpu.SemaphoreType.DMA((2,))],
        compiler_params=_params(("arbitrary", "arbitrary")),
        name="moe_experts",
    )(idx3, idx3, idx3, h2, wr, wg, wu, wd)


def _combine_body(cap, rc, idx_ref, ye_ref, x_hbm, mod_ref, ng_ref, acc_ref, xbuf, sem):
    bb = pl.program_id(0)
    e = pl.program_id(1)
    n = acc_ref.shape[1]

    @pl.when(e == 0)
    def _():
        acc_ref[...] = jnp.zeros(acc_ref.shape, F32)

    def body(g, carry):
        ts = [idx_ref[0, 0, g * COMBINE_GROUP + r] for r in range(COMBINE_GROUP)]
        rows = [acc_ref[0, pl.ds(ts[r], 1), :] + ye_ref[0, 0, pl.ds(g * COMBINE_GROUP + r, 1), :]
                for r in range(COMBINE_GROUP)]
        for r in range(COMBINE_GROUP):
            acc_ref[0, pl.ds(ts[r], 1), :] = rows[r]
        return carry
    lax.fori_loop(0, cap // COMBINE_GROUP, body, 0)

    @pl.when(e == pl.num_programs(1) - 1)
    def _():
        def x_copy(c, slot):
            return pltpu.make_async_copy(x_hbm.at[bb, pl.ds(c * rc, rc), :], xbuf.at[slot], sem.at[slot])

        x_copy(0, 0).start()
        for c in range(n // rc):
            slot = c % 2
            if c + 1 < n // rc:
                x_copy(c + 1, 1 - slot).start()
            x_copy(c, slot).wait()
            y = acc_ref[0, c * rc:(c + 1) * rc, :]
            yn = y * lax.rsqrt(jnp.mean(y * y, axis=-1, keepdims=True) + RMS_EPS) * ng_ref[...]
            acc_ref[0, c * rc:(c + 1) * rc, :] = xbuf[slot] + mod_ref[0, 5:6, :] * yn


def moe_combine_residual(ye, idx, x_mid, modv, ng3):
    b, ne, cap, d = ye.shape
    n = x_mid.shape[1]
    cap_pad = idx.shape[2]
    idx3 = idx.reshape(b * ne, 1, cap_pad)
    rc = min(COMBINE_X_ROWS, n)
    assert n * d * 4 <= V7X_VMEM_BYTES // 2 and n % rc == 0
    return pl.pallas_call(
        functools.partial(_combine_body, cap, rc),
        grid=(b, ne),
        in_specs=[
            pl.BlockSpec((1, 1, cap_pad), lambda bb, e: (bb * ne + e, 0, 0), memory_space=pltpu.SMEM),
            pl.BlockSpec((1, 1, cap, d), lambda bb, e: (bb, e, 0, 0)),
            pl.BlockSpec(memory_space=pl.ANY),
            pl.BlockSpec((1, 6, d), lambda bb, e: (bb, 0, 0)),
            pl.BlockSpec((1, d), lambda bb, e: (0, 0)),
        ],
        out_specs=pl.BlockSpec((1, n, d), lambda bb, e: (bb, 0, 0), pipeline_mode=pl.Buffered(1)),
        out_shape=jax.ShapeDtypeStruct((b, n, d), F32),
        scratch_shapes=[pltpu.VMEM((2, rc, d), F32), pltpu.SemaphoreType.DMA((2,))],
        compiler_params=_params(("arbitrary", "arbitrary")),
        name="moe_combine_residual",
    )(idx3, ye, x_mid, modv, ng3)


def _rope_tables(n):
    rows = n // GRID_W
    row = jnp.repeat(jnp.arange(rows), GRID_W).astype(F32)
    col = jnp.tile(jnp.arange(GRID_W), rows).astype(F32)
    inv = ROPE_THETA ** (-jnp.arange(ROPE_FREQS, dtype=F32) / ROPE_FREQS)
    ar, ac = row[:, None] * inv, col[:, None] * inv
    cos64 = jnp.concatenate([jnp.cos(ar), jnp.cos(ar), jnp.cos(ac), jnp.cos(ac)], axis=1)
    sin64 = jnp.concatenate([-jnp.sin(ar), jnp.sin(ar), -jnp.sin(ac), jnp.sin(ac)], axis=1)
    return jnp.tile(cos64, (1, 2)), jnp.tile(sin64, (1, 2))


def _moe_sublayer(x_mid, h2, probs_t, modv, ng3, wr, wg, wu, wd, layer, tm):
    n = x_mid.shape[1]
    cap = CAP_FACTOR * n // N_EXP
    idx = expert_topk(probs_t, cap)
    ye = moe_experts(h2, idx, cap, wr, wg, wu, wd, layer)
    return moe_combine_residual(ye, idx, x_mid, modv, ng3)


def _forward(x, c, ctx, c_ctx, w_mod, b_mod, norm_g, w_in, diff_lambda, diff_subln_g, s5_lam_re, s5_lam_im,
             s5_log_dt, s5_b_re, s5_b_im, s5_c_re, s5_c_im, s5_d, w_s5_glu, gmlp_norm_g, gmlp_ws, gmlp_bs,
             w_br_att, w_br_gmlp, w_out, w_router, w_e_gate, w_e_up, w_e_down, *, tm, tq, tk, t_chunk, hpb):
    b, n, d = x.shape
    n_ctx = ctx.shape[1]
    depth = w_mod.shape[0]
    tm_c = min(tm, n_ctx)
    tq_c = min(tq, n_ctx)

    c_rows = jnp.zeros((MOD_ROWS, d), F32).at[:b].set(c).at[b].set(c_ctx)
    mod_all = modulation(c_rows, w_mod, b_mod).reshape(depth, MOD_ROWS, 6, d)
    cos_t, sin_t = _rope_tables(n)
    cos_c, sin_c = jnp.ones((n_ctx, 128), F32), jnp.zeros((n_ctx, 128), F32)

    wg = w_e_gate.astype(BF16).reshape((depth * N_EXP,) + w_e_gate.shape[2:])
    wu = w_e_up.astype(BF16).reshape((depth * N_EXP,) + w_e_up.shape[2:])
    wd = w_e_down.astype(BF16).reshape((depth * N_EXP,) + w_e_down.shape[2:])

    xc = ctx
    for l in range(depth):
        last = l == depth - 1
        lam_init = 0.8 - 0.6 * math.exp(-0.3 * l)
        modv = mod_all[l, :b]
        modc = jnp.broadcast_to(mod_all[l, b:b + 1], (b, 6, d))
        ng0 = norm_g[l, 0:1]
        w_in_b = w_in[l].astype(BF16)
        gg = gmlp_norm_g[l][None, :]
        ws = gmlp_ws[l].astype(BF16)
        bs = jnp.repeat(gmlp_bs[l].T, GM_HD, axis=1)
        sg = diff_subln_g[l][None, :]
        wr = w_router[l].astype(BF16)

        q, k, v, u, ogm, gates = input_projection(x, modv, ng0, w_in_b, cos_t, sin_t, gg, ws, bs, True, tm)
        pc = input_projection(xc, modc, ng0, w_in_b, cos_c, sin_c, gg, ws, bs, not last, tm_c)
        if last:
            kc, vc, uc = pc
        else:
            qc, kc, vc, uc, ogm_c, gates_c = pc

        o_att = diff_attention(q, kc, vc, k, v, diff_lambda[l], sg, lam_init, tq, tk, hpb)

        wb, a, wc = s5_weights(s5_lam_re[l], s5_lam_im[l], s5_log_dt[l], s5_b_re[l], s5_b_im[l],
                               s5_c_re[l], s5_c_im[l])
        y5 = s5_scan(u, uc, b, wb, a, wc, s5_d[l][None, :], min(t_chunk, n_ctx))

        merge_w = (norm_g[l, 1:3], w_br_att[l].astype(BF16), w_s5_glu[l].astype(BF16),
                   w_br_gmlp[l].astype(BF16), w_out[l].astype(BF16), w_router[l].T.astype(BF16))
        x_mid, h2, probs_t = merge_branches(x, o_att, y5, 0, ogm, gates, modv, *merge_w, tm)
        x = _moe_sublayer(x_mid, h2, probs_t, modv, norm_g[l, 3:4], wr, wg, wu, wd, l, tm)
        if not last:
            o_att_c = diff_attention(qc, kc, vc, None, None, diff_lambda[l], sg, lam_init, tq_c, tk, hpb)
            xc_mid, h2c, probs_c = merge_branches(xc, o_att_c, y5, n, ogm_c, gates_c, modc, *merge_w, tm_c)
            xc = _moe_sublayer(xc_mid, h2c, probs_c, modc, norm_g[l, 3:4], wr, wg, wu, wd, l, tm_c)
    return x


def kernel(x, c, ctx, c_ctx, w_mod, b_mod, norm_g, w_in, diff_lambda, diff_subln_g, s5_lam_re, s5_lam_im, s5_log_dt, s5_b_re, s5_b_im, s5_c_re, s5_c_im, s5_d, w_s5_glu, gmlp_norm_g, gmlp_ws, gmlp_bs, w_br_att, w_br_gmlp, w_out, w_router, w_e_gate, w_e_up, w_e_down):
    return _forward(x, c, ctx, c_ctx, w_mod, b_mod, norm_g, w_in, diff_lambda, diff_subln_g, s5_lam_re, s5_lam_im,
                    s5_log_dt, s5_b_re, s5_b_im, s5_c_re, s5_c_im, s5_d, w_s5_glu, gmlp_norm_g, gmlp_ws, gmlp_bs,
                    w_br_att, w_br_gmlp, w_out, w_router, w_e_gate, w_e_up, w_e_down,
                    tm=512, tq=512, tk=512, t_chunk=256, hpb=4)
```
